```python
import jax, jax.numpy as jnp
from jax import lax
import numpy as np

D_MODEL = 1024
BATCH = 8
SEQ = 2048
DEPTH = 2
DEC_BATCH = 128
DEC_SEQ = 8
PAST_LEN = 16384
PAGE_SIZE = 128

N_MIXERS = 2
POOL_WINDOWS = (2, 4, 8, 16)
N_POOL_GROUPS = len(POOL_WINDOWS)
POOL_GROUP = D_MODEL // N_POOL_GROUPS
POOL_BUF = max(POOL_WINDOWS) - 1
CHUNK = 128
D_SGU = D_MODEL
N_SGU_GROUPS = 4
SGU_GROUP = D_SGU // N_SGU_GROUPS
D_FF = 4 * D_MODEL
EPS = 1e-6
N_POOL_LAYERS = (DEPTH + 1) // 2
N_SGU_LAYERS = DEPTH // 2

kernel_name = "pool_sgu_hybrid_decode_step"


def rms_norm(x, g):
    xf = x.astype(jnp.float32)
    y = xf * lax.rsqrt(jnp.mean(xf * xf, axis=-1, keepdims=True) + EPS)
    return (y * g.astype(jnp.float32)).astype(x.dtype)


def layer_norm(x, g, b):
    xf = x.astype(jnp.float32)
    mu = jnp.mean(xf, axis=-1, keepdims=True)
    xc = xf - mu
    var = jnp.mean(xc * xc, axis=-1, keepdims=True)
    return (xc * lax.rsqrt(var + EPS) * g.astype(jnp.float32) + b.astype(jnp.float32)).astype(x.dtype)


def ada_modulation(c, w_ada, b_ada):
    m = jax.nn.silu(c) @ w_ada + b_ada
    return jnp.split(m[:, None, :], 6, axis=-1)


def modulate(x, g, shift, scale):
    return rms_norm(x, g) * (1 + scale) + shift


def pool_mixer(h, buf, pos0, w_pool, pool_scale):
    B, L, D = h.shape
    xx = jnp.concatenate([buf.astype(h.dtype), h], axis=1)
    cs = jnp.cumsum(xx.astype(jnp.float32), axis=1)
    cs = jnp.pad(cs, ((0, 0), (1, 0), (0, 0)))
    pos = pos0 + jnp.arange(L)
    hi = cs[:, POOL_BUF + 1:POOL_BUF + 1 + L]
    means = []
    for gi, w in enumerate(POOL_WINDOWS):
        sl = slice(gi * POOL_GROUP, (gi + 1) * POOL_GROUP)
        lo = cs[:, POOL_BUF + 1 - w:POOL_BUF + 1 - w + L, sl]
        cnt = jnp.minimum(w, pos + 1).astype(jnp.float32)[None, :, None]
        means.append((hi[:, :, sl] - lo) / cnt)
    mean = jnp.concatenate(means, axis=-1)
    d = (mean - h.astype(jnp.float32)).astype(h.dtype)
    d = d.reshape(B, L, N_POOL_GROUPS, POOL_GROUP)
    y = jnp.einsum('blgc,gcd->blgd', d, w_pool).reshape(B, L, D)
    return y * pool_scale, xx[:, -POOL_BUF:]


def sgu_mixer(h, w_in, b_in, ln_g, ln_b, w_sp, b_sp, w_out):
    B, L, _ = h.shape
    z = jax.nn.gelu(h @ w_in + b_in, approximate=False)
    u, v = jnp.split(z, 2, axis=-1)
    v = layer_norm(v, ln_g, ln_b)
    n_chunks = -(-L // CHUNK)
    pad = n_chunks * CHUNK - L
    vc = jnp.pad(v, ((0, 0), (0, pad), (0, 0))).reshape(B, n_chunks, CHUNK, N_SGU_GROUPS, SGU_GROUP)
    mask = jnp.tril(jnp.ones((CHUNK, CHUNK), dtype=bool))
    w_eff = jnp.where(mask[None], w_sp, 0).astype(vc.dtype)
    mixed = jnp.einsum('gts,bnsgc->bntgc', w_eff, vc) + b_sp.T[None, None, :, :, None]
    mixed = mixed.reshape(B, n_chunks * CHUNK, D_SGU)[:, :L]
    y = (u * mixed) @ w_out
    last_start = ((L - 1) // CHUNK) * CHUNK
    return y, v[:, last_start:]


def channel_mlp(h, w1, w2):
    a = jax.nn.relu(h @ w1)
    return (a * a) @ w2


def trunk(x, c, pool_buf, pos0, norm_g, w_ada, b_ada, w_pool, pool_scale,
          sgu_w_in, sgu_b_in, sgu_ln_g, sgu_ln_b, sgu_w_sp, sgu_b_sp, sgu_w_out,
          mlp_w1, mlp_w2, final_g):
    new_pool, new_v = [], []
    for i in range(DEPTH):
        sh1, sc1, g1, sh2, sc2, g2 = ada_modulation(c, w_ada[i], b_ada[i])
        h = modulate(x, norm_g[i, 0], sh1, sc1)
        j = i // N_MIXERS
        if i % N_MIXERS == 0:
            buf = pool_buf[j] if pool_buf is not None else jnp.zeros((x.shape[0], POOL_BUF, x.shape[2]), x.dtype)
            y, nb = pool_mixer(h, buf, pos0, w_pool[j], pool_scale[j])
            new_pool.append(nb)
        else:
            y, vr = sgu_mixer(h, sgu_w_in[j], sgu_b_in[j], sgu_ln_g[j], sgu_ln_b[j],
                              sgu_w_sp[j], sgu_b_sp[j], sgu_w_out[j])
            new_v.append(vr)
        x = x + g1 * y
        h = modulate(x, norm_g[i, 1], sh2, sc2)
        x = x + g2 * channel_mlp(h, mlp_w1[i], mlp_w2[i])
    return rms_norm(x, final_g), jnp.stack(new_pool), jnp.stack(new_v)


def setup_inputs(seed: int = 0) -> dict:
    key = jax.random.key(seed)
    ks = jax.random.split(key, 24)
    f32 = jnp.float32
    nrm = lambda k, s, scale: jax.random.normal(k, s, f32) * scale
    return {
        "x_prompt": nrm(ks[0], (BATCH, SEQ, D_MODEL), 1.0),
        "x_sample": nrm(ks[1], (DEC_BATCH, DEC_SEQ, D_MODEL), 1.0),
        "c_prompt": nrm(ks[2], (BATCH, D_MODEL), 1.0),
        "c_sample": nrm(ks[3], (DEC_BATCH, D_MODEL), 1.0),
        "state_pool": nrm(ks[4], (N_POOL_LAYERS, DEC_BATCH, POOL_BUF, D_MODEL), 1.0),
        "norm_g": 1.0 + nrm(ks[5], (DEPTH, 2, D_MODEL), 0.05),
        "w_ada": nrm(ks[6], (DEPTH, D_MODEL, 6 * D_MODEL), 0.5 * D_MODEL ** -0.5),
        "b_ada": nrm(ks[7], (DEPTH, 6 * D_MODEL), 0.01),
        "w_pool": nrm(ks[8], (N_POOL_LAYERS, N_POOL_GROUPS, POOL_GROUP, POOL_GROUP), POOL_GROUP ** -0.5),
        "pool_scale": 1.0 + nrm(ks[9], (N_POOL_LAYERS, D_MODEL), 0.05),
        "sgu_w_in": nrm(ks[10], (N_SGU_LAYERS, D_MODEL, 2 * D_SGU), D_MODEL ** -0.5),
        "sgu_b_in": nrm(ks[11], (N_SGU_LAYERS, 2 * D_SGU), 0.01),
        "sgu_ln_g": 1.0 + nrm(ks[12], (N_SGU_LAYERS, D_SGU), 0.05),
        "sgu_ln_b": nrm(ks[13], (N_SGU_LAYERS, D_SGU), 0.01),
        "sgu_w_sp": nrm(ks[14], (N_SGU_LAYERS, N_SGU_GROUPS, CHUNK, CHUNK), CHUNK ** -0.5),
        "sgu_b_sp": 1.0 + nrm(ks[15], (N_SGU_LAYERS, N_SGU_GROUPS, CHUNK), 0.05),
        "sgu_w_out": nrm(ks[16], (N_SGU_LAYERS, D_SGU, D_MODEL), D_SGU ** -0.5),
        "mlp_w1": nrm(ks[17], (DEPTH, D_MODEL, D_FF), D_MODEL ** -0.5),
        "mlp_w2": nrm(ks[18], (DEPTH, D_FF, D_MODEL), D_FF ** -0.5),
        "final_g": 1.0 + nrm(ks[19], (D_MODEL,), 0.05),
    }


def reference(x_prompt, x_sample, c_prompt, c_sample, state_pool, norm_g, w_ada, b_ada,
              w_pool, pool_scale, sgu_w_in, sgu_b_in, sgu_ln_g, sgu_ln_b, sgu_w_sp,
              sgu_b_sp, sgu_w_out, mlp_w1, mlp_w2, final_g):
    y_prompt, new_pool_prompt, new_sgu_v_prompt = trunk(
        x_prompt, c_prompt, None, 0, norm_g, w_ada, b_ada, w_pool, pool_scale,
        sgu_w_in, sgu_b_in, sgu_ln_g, sgu_ln_b, sgu_w_sp, sgu_b_sp, sgu_w_out,
        mlp_w1, mlp_w2, final_g)
    y_sample, new_pool_sample, new_sgu_v_sample = trunk(
        x_sample, c_sample, state_pool, PAST_LEN, norm_g, w_ada, b_ada, w_pool, pool_scale,
        sgu_w_in, sgu_b_in, sgu_ln_g, sgu_ln_b, sgu_w_sp, sgu_b_sp, sgu_w_out,
        mlp_w1, mlp_w2, final_g)
    return (y_prompt, y_sample, new_pool_prompt, new_pool_sample, new_sgu_v_prompt, new_sgu_v_sample)
```

```python
import functools

import jax
import jax.numpy as jnp
from jax import lax
from jax.experimental import pallas as pl
from jax.experimental.pallas import tpu as pltpu

F32 = jnp.float32
BF16 = jnp.bfloat16

POOL_WINDOWS = (2, 4, 8, 16)
POOL_BUF = max(POOL_WINDOWS) - 1
HIST = 16
CHUNK = 128
N_SGU_GROUPS = 4
N_MOD = 6
EPS = 1e-6

TILE_ROWS = 512
SAMPLE_TILE_ROWS = 256
FF_COLS = 512
ADA_COLS = 2048
VMEM_LIMIT = 56 * 1024 * 1024


def _rms_norm(x, g):
    ms = jnp.mean(x * x, axis=-1, keepdims=True)
    return x * lax.rsqrt(ms + EPS) * g


def _modulate(x, g, shift, scale):
    return _rms_norm(x, g) * (1 + scale) + shift


def _gelu(z):
    return 0.5 * z * (1 + lax.erf(z * (0.5 ** 0.5)))


def _resident(shape):
    return pl.BlockSpec(shape, lambda *_: (0,) * len(shape), pipeline_mode=pl.Buffered(1))


def _mod_parts(mod_ref, modx_ref, seq_rows, d):
    if modx_ref is None:
        return [mod_ref[:, k * d:(k + 1) * d] for k in range(N_MOD)]
    n_seq = mod_ref.shape[0]

    def body(b, carry):
        row = mod_ref[pl.ds(b, 1), :]
        r0 = pl.multiple_of(b * seq_rows, seq_rows)
        for k in range(N_MOD):
            modx_ref[k, pl.ds(r0, seq_rows), :] = jnp.broadcast_to(row[:, k * d:(k + 1) * d], (seq_rows, d))
        return carry

    lax.fori_loop(0, n_seq, body, 0)
    return [modx_ref[k] for k in range(N_MOD)]


def _channel_mlp(xn, w1_ref, w2_ref, a_ref):
    d_ff = w1_ref.shape[1]
    for c in range(d_ff // FF_COLS):
        cols = slice(c * FF_COLS, (c + 1) * FF_COLS)
        a = jnp.maximum(jnp.dot(xn, w1_ref[:, cols], preferred_element_type=F32), 0.0)
        a_ref[:, cols] = (a * a).astype(BF16)
    return jnp.dot(a_ref[...], w2_ref[...], preferred_element_type=F32)


def _ada_kernel(c_ref, w_ref, b_ref, o_ref):
    s = jax.nn.silu(c_ref[...]).astype(BF16)
    o_ref[...] = jnp.dot(s, w_ref[...].astype(BF16), preferred_element_type=F32) + b_ref[...]


def _ada_modulation(c_all, w_ada, b_ada):
    depth, d, n = w_ada.shape
    rows = c_all.shape[0]
    return pl.pallas_call(
        _ada_kernel,
        grid=(depth, n // ADA_COLS),
        in_specs=[
            pl.BlockSpec((rows, d), lambda l, j: (0, 0)),
            pl.BlockSpec((None, d, ADA_COLS), lambda l, j: (l, 0, j)),
            pl.BlockSpec((None, 1, ADA_COLS), lambda l, j: (l, 0, j)),
        ],
        out_specs=pl.BlockSpec((None, rows, ADA_COLS), lambda l, j: (l, 0, j)),
        out_shape=jax.ShapeDtypeStruct((depth, rows, n), F32),
        compiler_params=pltpu.CompilerParams(
            dimension_semantics=("arbitrary", "arbitrary"), vmem_limit_bytes=VMEM_LIMIT),
        name="ada_modulation",
    )(c_all, w_ada, b_ada.reshape(depth, 1, n))


def _pool_delta(xx, h, take, first_tile):
    pg = h.shape[1] // len(POOL_WINDOWS)
    parts = []
    for g, w in enumerate(POOL_WINDOWS):
        cols = slice(g * pg, (g + 1) * pg)
        s = xx[:, cols]
        span = 1
        while span < w:
            s = s + pltpu.roll(s, span, axis=0)
            span *= 2
        s = take(s)
        m = s * (1.0 / w)
        if first_tile is not None:
            r = lax.broadcasted_iota(jnp.int32, (HIST, pg), 0)
            cnt = jnp.where(first_tile, jnp.minimum(w, r + 1), w).astype(F32)
            m = jnp.concatenate([s[:HIST] / cnt, m[HIST:]], axis=0)
        parts.append(m - h[:, cols])
    return parts


def _layer0_kernel(seq_rows, *refs):
    if seq_rows is None:
        (x_ref, mod_ref, ng_ref, wp_ref, ps_ref, w1_ref, w2_ref,
         o_ref, tail_ref, hist_ref, a_ref) = refs
        buf_ref = modx_ref = None
    else:
        (x_ref, buf_ref, mod_ref, ng_ref, wp_ref, ps_ref, w1_ref, w2_ref,
         o_ref, h_ref, modx_ref, a_ref) = refs
    tm, d = x_ref.shape
    sh1, sc1, gt1, sh2, sc2, gt2 = _mod_parts(mod_ref, modx_ref, seq_rows, d)

    x = x_ref[...]
    h = _modulate(x, ng_ref[0:1, :], sh1, sc1)
    if seq_rows is None:
        j = pl.program_id(1)

        @pl.when(j == 0)
        def _():
            hist_ref[...] = jnp.zeros_like(hist_ref)

        xx = jnp.concatenate([hist_ref[...], h], axis=0)
        hist_ref[...] = h[tm - HIST:]

        @pl.when(j == pl.num_programs(1) - 1)
        def _():
            tail_ref[...] = h[tm - HIST:]

        delta = _pool_delta(xx, h, lambda s: s[HIST:], j == 0)
    else:
        h_ref[...] = h
        n_seq = tm // seq_rows
        xx = jnp.concatenate(
            [buf_ref[...].reshape(n_seq, HIST, d), h.reshape(n_seq, seq_rows, d)], axis=1
        ).reshape(n_seq * (HIST + seq_rows), d)

        def take(s):
            return s.reshape(n_seq, HIST + seq_rows, s.shape[1])[:, HIST:, :].reshape(tm, s.shape[1])

        delta = _pool_delta(xx, h, take, None)

    pg = d // len(POOL_WINDOWS)
    ys = [jnp.dot(dl.astype(BF16), wp_ref[g], preferred_element_type=F32) for g, dl in enumerate(delta)]
    y = jnp.concatenate(ys, axis=1) * ps_ref[...]
    x1 = x + gt1 * y

    xn = _modulate(x1, ng_ref[1:2, :], sh2, sc2).astype(BF16)
    o_ref[...] = x1 + gt2 * _channel_mlp(xn, w1_ref, w2_ref, a_ref)


def _layer0_prompt(x, mod, ng, wp, ps, w1, w2):
    n_seq, seq, d = x.shape
    d_ff = w1.shape[1]
    tm = TILE_ROWS
    return pl.pallas_call(
        functools.partial(_layer0_kernel, None),
        grid=(n_seq, seq // tm),
        in_specs=[
            pl.BlockSpec((None, tm, d), lambda b, j: (b, j, 0)),
            pl.BlockSpec((None, 1, N_MOD * d), lambda b, j: (b, 0, 0)),
            _resident(ng.shape), _resident(wp.shape), _resident(ps.shape),
            _resident(w1.shape), _resident(w2.shape),
        ],
        out_specs=[
            pl.BlockSpec((None, tm, d), lambda b, j: (b, j, 0)),
            pl.BlockSpec((None, HIST, d), lambda b, j: (b, 0, 0)),
        ],
        out_shape=[
            jax.ShapeDtypeStruct((n_seq, seq, d), F32),
            jax.ShapeDtypeStruct((n_seq, HIST, d), F32),
        ],
        scratch_shapes=[pltpu.VMEM((HIST, d), F32), pltpu.VMEM((tm, d_ff), BF16)],
        compiler_params=pltpu.CompilerParams(
            dimension_semantics=("arbitrary", "arbitrary"), vmem_limit_bytes=VMEM_LIMIT),
        name="layer0_prompt",
    )(x, mod.reshape(n_seq, 1, N_MOD * d), ng, wp, ps, w1, w2)


def _layer0_sample(x, buf, mod, ng, wp, ps, w1, w2, seq_rows):
    n_tok, d = x.shape
    d_ff = w1.shape[1]
    tm = SAMPLE_TILE_ROWS
    tb = tm // seq_rows
    return pl.pallas_call(
        functools.partial(_layer0_kernel, seq_rows),
        grid=(n_tok // tm,),
        in_specs=[
            pl.BlockSpec((tm, d), lambda i: (i, 0)),
            pl.BlockSpec((tb * HIST, d), lambda i: (i, 0)),
            pl.BlockSpec((tb, N_MOD * d), lambda i: (i, 0)),
            _resident(ng.shape), _resident(wp.shape), _resident(ps.shape),
            _resident(w1.shape), _resident(w2.shape),
        ],
        out_specs=[pl.BlockSpec((tm, d), lambda i: (i, 0)), pl.BlockSpec((tm, d), lambda i: (i, 0))],
        out_shape=[jax.ShapeDtypeStruct((n_tok, d), F32), jax.ShapeDtypeStruct((n_tok, d), F32)],
        scratch_shapes=[pltpu.VMEM((N_MOD, tm, d), F32), pltpu.VMEM((tm, d_ff), BF16)],
        compiler_params=pltpu.CompilerParams(
            dimension_semantics=("arbitrary",), vmem_limit_bytes=VMEM_LIMIT),
        name="layer0_sample",
    )(x, buf, mod, ng, wp, ps, w1, w2)


def _layer1_kernel(seq_rows, v_rows, *refs):
    (x_ref, mod_ref, ng_ref, win_ref, bin_ref, lng_ref, lnb_ref, wsp_ref, bsp_ref, wout_ref,
     w1_ref, w2_ref, fg_ref, o_ref, v_ref) = refs[:15]
    if seq_rows is None:
        z_ref, gate_ref, a_ref = refs[15:]
        modx_ref = None
    else:
        modx_ref, z_ref, gate_ref, a_ref = refs[15:]
    tm, d = x_ref.shape
    ds = wout_ref.shape[0]
    sh1, sc1, gt1, sh2, sc2, gt2 = _mod_parts(mod_ref, modx_ref, seq_rows, d)

    x = x_ref[...]
    h = _modulate(x, ng_ref[0:1, :], sh1, sc1).astype(BF16)
    for c in range(2 * ds // FF_COLS):
        cols = slice(c * FF_COLS, (c + 1) * FF_COLS)
        z = jnp.dot(h, win_ref[:, cols], preferred_element_type=F32) + bin_ref[:, cols]
        z_ref[:, cols] = _gelu(z)

    v = z_ref[:, ds:]
    mu = jnp.mean(v, axis=-1, keepdims=True)
    vc = v - mu
    var = jnp.mean(vc * vc, axis=-1, keepdims=True)
    vn = vc * lax.rsqrt(var + EPS) * lng_ref[...] + lnb_ref[...]
    if seq_rows is not None:
        v_ref[...] = vn
    else:
        @pl.when(pl.program_id(1) == pl.num_programs(1) - 1)
        def _():
            v_ref[...] = vn[tm - v_rows:]
    vnb = vn.astype(BF16)

    chunk_rows = CHUNK if seq_rows is None else seq_rows
    r = lax.broadcasted_iota(jnp.int32, (CHUNK, CHUNK), 0)
    c = lax.broadcasted_iota(jnp.int32, (CHUNK, CHUNK), 1)
    causal = (c <= r) & ((r // chunk_rows) == (c // chunk_rows))
    gs = ds // N_SGU_GROUPS
    for g in range(N_SGU_GROUPS):
        wt = jnp.where(causal, wsp_ref[g], 0.0).astype(BF16)
        cols = slice(g * gs, (g + 1) * gs)
        for ci in range(tm // CHUNK):
            rows = slice(ci * CHUNK, (ci + 1) * CHUNK)
            mixed = jnp.dot(wt, vnb[rows, cols], preferred_element_type=F32) + bsp_ref[:, cols]
            gate_ref[rows, cols] = (z_ref[rows, cols] * mixed).astype(BF16)
    y = jnp.dot(gate_ref[...], wout_ref[...], preferred_element_type=F32)
    x1 = x + gt1 * y

    xn = _modulate(x1, ng_ref[1:2, :], sh2, sc2).astype(BF16)
    x2 = x1 + gt2 * _channel_mlp(xn, w1_ref, w2_ref, a_ref)
    o_ref[...] = _rms_norm(x2, fg_ref[...])


def _layer1_weight_specs(weights):
    return [_resident(w.shape) for w in weights]


def _layer1_prompt(x, mod, weights):
    n_seq, seq, d = x.shape
    ds = weights[7].shape[0]
    d_ff = weights[8].shape[1]
    tm = TILE_ROWS
    v_rows = seq - CHUNK * ((seq - 1) // CHUNK)
    return pl.pallas_call(
        functools.partial(_layer1_kernel, None, v_rows),
        grid=(n_seq, seq // tm),
        in_specs=[
            pl.BlockSpec((None, tm, d), lambda b, j: (b, j, 0)),
            pl.BlockSpec((None, 1, N_MOD * d), lambda b, j: (b, 0, 0)),
        ] + _layer1_weight_specs(weights),
        out_specs=[
            pl.BlockSpec((None, tm, d), lambda b, j: (b, j, 0)),
            pl.BlockSpec((None, v_rows, ds), lambda b, j: (b, 0, 0)),
        ],
        out_shape=[
            jax.ShapeDtypeStruct((n_seq, seq, d), F32),
            jax.ShapeDtypeStruct((n_seq, v_rows, ds), F32),
        ],
        scratch_shapes=[
            pltpu.VMEM((tm, 2 * ds), F32), pltpu.VMEM((tm, ds), BF16), pltpu.VMEM((tm, d_ff), BF16)],
        compiler_params=pltpu.CompilerParams(
            dimension_semantics=("arbitrary", "arbitrary"), vmem_limit_bytes=VMEM_LIMIT),
        name="layer1_prompt",
    )(x, mod.reshape(n_seq, 1, N_MOD * d), *weights)


def _layer1_sample(x, mod, weights, seq_rows):
    n_tok, d = x.shape
    ds = weights[7].shape[0]
    d_ff = weights[8].shape[1]
    tm = SAMPLE_TILE_ROWS
    tb = tm // seq_rows
    return pl.pallas_call(
        functools.partial(_layer1_kernel, seq_rows, tm),
        grid=(n_tok // tm,),
        in_specs=[
            pl.BlockSpec((tm, d), lambda i: (i, 0)),
            pl.BlockSpec((tb, N_MOD * d), lambda i: (i, 0)),
        ] + _layer1_weight_specs(weights),
        out_specs=[pl.BlockSpec((tm, d), lambda i: (i, 0)), pl.BlockSpec((tm, ds), lambda i: (i, 0))],
        out_shape=[jax.ShapeDtypeStruct((n_tok, d), F32), jax.ShapeDtypeStruct((n_tok, ds), F32)],
        scratch_shapes=[
            pltpu.VMEM((N_MOD, tm, d), F32),
            pltpu.VMEM((tm, 2 * ds), F32), pltpu.VMEM((tm, ds), BF16), pltpu.VMEM((tm, d_ff), BF16)],
        compiler_params=pltpu.CompilerParams(
            dimension_semantics=("arbitrary",), vmem_limit_bytes=VMEM_LIMIT),
        name="layer1_sample",
    )(x, mod, *weights)


def kernel(x_prompt, x_sample, c_prompt, c_sample, state_pool, norm_g, w_ada, b_ada, w_pool, pool_scale,
           sgu_w_in, sgu_b_in, sgu_ln_g, sgu_ln_b, sgu_w_sp, sgu_b_sp, sgu_w_out, mlp_w1, mlp_w2, final_g):
    n_p, seq, d = x_prompt.shape
    n_s, dec_seq, _ = x_sample.shape
    depth = norm_g.shape[0]
    assert depth == 2 and state_pool.shape[0] == 1 and sgu_w_in.shape[0] == 1
    assert state_pool.shape[2] == POOL_BUF and dec_seq < POOL_BUF and dec_seq <= CHUNK
    assert seq % TILE_ROWS == 0 and TILE_ROWS % CHUNK == 0
    assert (n_s * dec_seq) % SAMPLE_TILE_ROWS == 0 and SAMPLE_TILE_ROWS % CHUNK == 0

    mod = _ada_modulation(jnp.concatenate([c_prompt, c_sample], axis=0), w_ada, b_ada)
    mod_p, mod_s = mod[:, :n_p], mod[:, n_p:]

    w0 = (norm_g[0], w_pool[0].astype(BF16), pool_scale[0:1], mlp_w1[0].astype(BF16), mlp_w2[0].astype(BF16))
    xp, tail_p = _layer0_prompt(x_prompt, mod_p[0], *w0)
    buf = jnp.pad(state_pool[0], ((0, 0), (HIST - POOL_BUF, 0), (0, 0))).reshape(n_s * HIST, d)
    xs, h_s = _layer0_sample(x_sample.reshape(n_s * dec_seq, d), buf, mod_s[0], *w0, seq_rows=dec_seq)
    new_pool_prompt = tail_p[None, :, HIST - POOL_BUF:]
    new_pool_sample = jnp.concatenate(
        [state_pool[0][:, dec_seq:], h_s.reshape(n_s, dec_seq, d)], axis=1)[None]

    ds = sgu_w_out.shape[1]
    gs = ds // N_SGU_GROUPS
    bias = jnp.repeat(sgu_b_sp[0].T, gs, axis=1)
    common = (norm_g[1], sgu_w_in[0].astype(BF16), sgu_b_in[0:1], sgu_ln_g[0:1], sgu_ln_b[0:1])
    tail = (sgu_w_out[0].astype(BF16), mlp_w1[1].astype(BF16), mlp_w2[1].astype(BF16), final_g[None])
    rep = CHUNK // dec_seq
    w_sp_s = jnp.tile(sgu_w_sp[0][:, :dec_seq, :dec_seq], (1, rep, rep))
    bias_s = jnp.tile(bias[:dec_seq], (rep, 1))
    yp, v_p = _layer1_prompt(xp, mod_p[1], common + (sgu_w_sp[0], bias) + tail)
    ys, v_s = _layer1_sample(xs, mod_s[1], common + (w_sp_s, bias_s) + tail, seq_rows=dec_seq)

    return (yp, ys.reshape(n_s, dec_seq, d), new_pool_prompt, new_pool_sample,
            v_p[None], v_s.reshape(1, n_s, dec_seq, ds))
```

```python
import functools

import jax
import jax.numpy as jnp
from jax import lax
from jax.experimental import pallas as pl
from jax.experimental.pallas import tpu as pltpu

F32 = jnp.float32
BF16 = jnp.bfloat16

POOL_WINDOWS = (2, 4, 8, 16)
POOL_BUF = max(POOL_WINDOWS) - 1
HIST = 16
CHUNK = 128
N_SGU_GROUPS = 4
N_MOD = 6
EPS = 1e-6

TILE_ROWS = 512
SAMPLE_SEQS = 64
FF_COLS = 512
ADA_COLS = 2048
VMEM_LIMIT = 56 * 1024 * 1024


def _rms_norm(x, g):
    ms = jnp.mean(x * x, axis=-1, keepdims=True)
    return x * lax.rsqrt(ms + EPS) * g


def _modulate(x, g, shift, scale):
    return _rms_norm(x, g) * (1 + scale) + shift


def _gelu(z):
    return 0.5 * z * (1 + lax.erf(z * (0.5 ** 0.5)))


def _layer_norm(v, g, b):
    mu = jnp.mean(v, axis=-1, keepdims=True)
    vc = v - mu
    var = jnp.mean(vc * vc, axis=-1, keepdims=True)
    return vc * lax.rsqrt(var + EPS) * g + b


def _resident(shape, layer):
    if len(shape) == 2:
        assert shape[0] == 1 and layer == 0
        return pl.BlockSpec(tuple(shape), lambda *_: (0, 0), pipeline_mode=pl.Buffered(1))
    return pl.BlockSpec((None,) + tuple(shape[1:]), lambda *_: (layer,) + (0,) * (len(shape) - 1),
                        pipeline_mode=pl.Buffered(1))


def _split_mod(mod, d):
    return [mod[:, k * d:(k + 1) * d] for k in range(N_MOD)]


def _channel_mlp(xn, w1_ref, w2_ref, a_ref):
    d_ff = w1_ref.shape[1]
    for c in range(d_ff // FF_COLS):
        cols = slice(c * FF_COLS, (c + 1) * FF_COLS)
        a = jnp.maximum(jnp.dot(xn, w1_ref[:, cols], preferred_element_type=F32), 0.0)
        a_ref[:, cols] = (a * a).astype(BF16)
    return jnp.dot(a_ref[...], w2_ref[...], preferred_element_type=F32)


def _pool_project(delta, wp_ref, ps_ref):
    ys = [jnp.dot(dl, wp_ref[g], preferred_element_type=F32) for g, dl in enumerate(delta)]
    return jnp.concatenate(ys, axis=1) * ps_ref[...]


def _sgu_in(h, win_ref, bin_ref, z_ref):
    for c in range(win_ref.shape[1] // FF_COLS):
        cols = slice(c * FF_COLS, (c + 1) * FF_COLS)
        z = jnp.dot(h, win_ref[:, cols], preferred_element_type=F32) + bin_ref[:, cols]
        z_ref[:, cols] = _gelu(z)


def _ada_kernel(c_ref, w_ref, b_ref, o_ref):
    s = jax.nn.silu(c_ref[...]).astype(BF16)
    b = b_ref[pl.ds(pl.program_id(0), 1), :]
    o_ref[...] = jnp.dot(s, w_ref[...].astype(BF16), preferred_element_type=F32) + b


def _ada_modulation(c_all, w_ada, b_ada):
    depth, d, n = w_ada.shape
    rows = c_all.shape[0]
    return pl.pallas_call(
        _ada_kernel,
        grid=(depth, n // ADA_COLS),
        in_specs=[
            pl.BlockSpec((rows, d), lambda l, j: (0, 0)),
            pl.BlockSpec((None, d, ADA_COLS), lambda l, j: (l, 0, j)),
            pl.BlockSpec((depth, ADA_COLS), lambda l, j: (0, j)),
        ],
        out_specs=pl.BlockSpec((None, rows, ADA_COLS), lambda l, j: (l, 0, j)),
        out_shape=jax.ShapeDtypeStruct((depth, rows, n), F32),
        compiler_params=pltpu.CompilerParams(
            dimension_semantics=("arbitrary", "arbitrary"), vmem_limit_bytes=VMEM_LIMIT),
        name="ada_modulation",
    )(c_all, w_ada, b_ada)


def _layer0_prompt_kernel(x_ref, mod_ref, ng_ref, wp_ref, ps_ref, w1_ref, w2_ref,
                          o_ref, tail_ref, hist_ref, a_ref):
    tm, d = x_ref.shape
    b, j = pl.program_id(0), pl.program_id(1)
    sh1, sc1, gt1, sh2, sc2, gt2 = _split_mod(mod_ref[pl.ds(b, 1), :], d)

    @pl.when(j == 0)
    def _():
        hist_ref[...] = jnp.zeros_like(hist_ref)

    x = x_ref[...]
    h = _modulate(x, ng_ref[0:1, :], sh1, sc1)
    xx = jnp.concatenate([hist_ref[...], h], axis=0)
    hist_ref[...] = h[tm - HIST:]

    @pl.when(j == pl.num_programs(1) - 1)
    def _():
        for k in range(POOL_BUF):
            tail_ref[k, pl.ds(b, 1), :] = h[tm - POOL_BUF + k:tm - POOL_BUF + k + 1]

    pg = d // len(POOL_WINDOWS)
    delta = []
    for g, w in enumerate(POOL_WINDOWS):
        cols = slice(g * pg, (g + 1) * pg)
        s = xx[:, cols]
        span = 1
        while span < w:
            s = s + pltpu.roll(s, span, axis=0)
            span *= 2
        s = s[HIST:]
        r = lax.broadcasted_iota(jnp.int32, (HIST, pg), 0)
        cnt = jnp.where(j == 0, jnp.minimum(w, r + 1), w).astype(F32)
        m = jnp.concatenate([s[:HIST] / cnt, s[HIST:] * (1.0 / w)], axis=0)
        delta.append((m - h[:, cols]).astype(BF16))
    x1 = x + gt1 * _pool_project(delta, wp_ref, ps_ref)

    xn = _modulate(x1, ng_ref[1:2, :], sh2, sc2).astype(BF16)
    o_ref[...] = x1 + gt2 * _channel_mlp(xn, w1_ref, w2_ref, a_ref)


def _layer0_sample_kernel(seq_rows, x_ref, buf_ref, mod_ref, ng_ref, wp_ref, ps_ref, w1_ref, w2_ref,
                          o_ref, pool_ref, a_ref):
    n_seq, _, d = x_ref.shape
    sh1, sc1, gt1, sh2, sc2, gt2 = _split_mod(mod_ref[...], d)

    xt = jnp.swapaxes(x_ref[...], 0, 1)
    xs = [xt[t] for t in range(seq_rows)]
    hs = [_modulate(x, ng_ref[0:1, :], sh1, sc1) for x in xs]
    rows = [buf_ref[k] for k in range(POOL_BUF)] + hs
    for k in range(POOL_BUF):
        pool_ref[k] = rows[len(rows) - POOL_BUF + k]

    pg = d // len(POOL_WINDOWS)
    delta = []
    for g, w in enumerate(POOL_WINDOWS):
        cols = slice(g * pg, (g + 1) * pg)
        sums = {i: rows[i][:, cols] for i in range(len(rows))}
        span = 1
        while span < w:
            first = POOL_BUF - (w - 2 * span)
            sums = {i: sums[i] + sums[i - span] for i in range(first, len(rows))}
            span *= 2
        delta.append(jnp.concatenate(
            [(sums[POOL_BUF + t] * (1.0 / w) - hs[t][:, cols]).astype(BF16) for t in range(seq_rows)], axis=0))
    y = _pool_project(delta, wp_ref, ps_ref)
    x1 = [xs[t] + gt1 * y[t * n_seq:(t + 1) * n_seq] for t in range(seq_rows)]

    xn = jnp.concatenate([_modulate(x, ng_ref[1:2, :], sh2, sc2).astype(BF16) for x in x1], axis=0)
    y = _channel_mlp(xn, w1_ref, w2_ref, a_ref)
    out = [x1[t] + gt2 * y[t * n_seq:(t + 1) * n_seq] for t in range(seq_rows)]
    o_ref[...] = jnp.swapaxes(jnp.stack(out, axis=0), 0, 1)


def _layer0_prompt(x, mod, n_lead, layer, ng, wp, ps, w1, w2):
    n_seq, seq, d = x.shape
    d_ff = w1.shape[2]
    tm = TILE_ROWS
    return pl.pallas_call(
        _layer0_prompt_kernel,
        grid=(n_seq, seq // tm),
        in_specs=[
            pl.BlockSpec((None, tm, d), lambda b, j: (b, j, 0)),
            pl.BlockSpec((None, n_seq, N_MOD * d), lambda b, j: (layer, n_lead // n_seq, 0)),
            _resident(ng.shape, layer), _resident(wp.shape, 0), _resident(ps.shape, 0),
            _resident(w1.shape, layer), _resident(w2.shape, layer),
        ],
        out_specs=[
            pl.BlockSpec((None, tm, d), lambda b, j: (b, j, 0)),
            pl.BlockSpec((POOL_BUF, n_seq, d), lambda b, j: (0, 0, 0)),
        ],
        out_shape=[
            jax.ShapeDtypeStruct((n_seq, seq, d), F32),
            jax.ShapeDtypeStruct((POOL_BUF, n_seq, d), F32),
        ],
        scratch_shapes=[pltpu.VMEM((HIST, d), F32), pltpu.VMEM((tm, d_ff), BF16)],
        compiler_params=pltpu.CompilerParams(
            dimension_semantics=("arbitrary", "arbitrary"), vmem_limit_bytes=VMEM_LIMIT),
        name="layer0_prompt",
    )(x, mod, ng, wp, ps, w1, w2)


def _layer0_sample(x, buf, mod, layer, ng, wp, ps, w1, w2):
    n_seq, seq_rows, d = x.shape
    d_ff = w1.shape[2]
    tb = SAMPLE_SEQS
    tm = tb * seq_rows
    return pl.pallas_call(
        functools.partial(_layer0_sample_kernel, seq_rows),
        grid=(n_seq // tb,),
        in_specs=[
            pl.BlockSpec((tb, seq_rows, d), lambda i: (i, 0, 0)),
            pl.BlockSpec((POOL_BUF, tb, d), lambda i: (0, i, 0)),
            pl.BlockSpec((None, tb, N_MOD * d), lambda i: (layer, i, 0)),
            _resident(ng.shape, layer), _resident(wp.shape, 0), _resident(ps.shape, 0),
            _resident(w1.shape, layer), _resident(w2.shape, layer),
        ],
        out_specs=[pl.BlockSpec((tb, seq_rows, d), lambda i: (i, 0, 0)),
                   pl.BlockSpec((POOL_BUF, tb, d), lambda i: (0, i, 0))],
        out_shape=[jax.ShapeDtypeStruct(x.shape, F32), jax.ShapeDtypeStruct(buf.shape, F32)],
        scratch_shapes=[pltpu.VMEM((tm, d_ff), BF16)],
        compiler_params=pltpu.CompilerParams(
            dimension_semantics=("arbitrary",), vmem_limit_bytes=VMEM_LIMIT),
        name="layer0_sample",
    )(x, buf, mod, ng, wp, ps, w1, w2)


def _layer1_prompt_kernel(x_ref, mod_ref, ng_ref, win_ref, bin_ref, lng_ref, lnb_ref, wsp_ref, bsp_ref,
                          wout_ref, w1_ref, w2_ref, fg_ref, o_ref, v_ref, z_ref, gate_ref, a_ref):
    tm, d = x_ref.shape
    ds = wout_ref.shape[0]
    v_rows = v_ref.shape[0]
    sh1, sc1, gt1, sh2, sc2, gt2 = _split_mod(mod_ref[pl.ds(pl.program_id(0), 1), :], d)

    x = x_ref[...]
    _sgu_in(_modulate(x, ng_ref[0:1, :], sh1, sc1).astype(BF16), win_ref, bin_ref, z_ref)
    vn = _layer_norm(z_ref[:, ds:], lng_ref[...], lnb_ref[...])

    @pl.when(pl.program_id(1) == pl.num_programs(1) - 1)
    def _():
        v_ref[...] = vn[tm - v_rows:]

    vnb = vn.astype(BF16)
    r = lax.broadcasted_iota(jnp.int32, (CHUNK, CHUNK), 0)
    c = lax.broadcasted_iota(jnp.int32, (CHUNK, CHUNK), 1)
    gs = ds // N_SGU_GROUPS
    for g in range(N_SGU_GROUPS):
        wt = jnp.where(c <= r, wsp_ref[g], 0.0).astype(BF16)
        cols = slice(g * gs, (g + 1) * gs)
        for ci in range(tm // CHUNK):
            rows = slice(ci * CHUNK, (ci + 1) * CHUNK)
            mixed = jnp.dot(wt, vnb[rows, cols], preferred_element_type=F32) + bsp_ref[:, cols]
            gate_ref[rows, cols] = (z_ref[rows, cols] * mixed).astype(BF16)
    x1 = x + gt1 * jnp.dot(gate_ref[...], wout_ref[...], preferred_element_type=F32)

    xn = _modulate(x1, ng_ref[1:2, :], sh2, sc2).astype(BF16)
    x2 = x1 + gt2 * _channel_mlp(xn, w1_ref, w2_ref, a_ref)
    o_ref[...] = _rms_norm(x2, fg_ref[...])


def _layer1_sample_kernel(seq_rows, wsp_ref, x_ref, mod_ref, ng_ref, win_ref, bin_ref, lng_ref, lnb_ref,
                          bsp_ref, wout_ref, w1_ref, w2_ref, fg_ref, o_ref, v_ref, z_ref, a_ref):
    n_seq, _, d = x_ref.shape
    ds = wout_ref.shape[0]
    sh1, sc1, gt1, sh2, sc2, gt2 = _split_mod(mod_ref[...], d)

    xt = jnp.swapaxes(x_ref[...], 0, 1)
    xs = [xt[t] for t in range(seq_rows)]
    h = jnp.concatenate([_modulate(x, ng_ref[0:1, :], sh1, sc1).astype(BF16) for x in xs], axis=0)
    _sgu_in(h, win_ref, bin_ref, z_ref)
    vn = _layer_norm(z_ref[:, ds:], lng_ref[...], lnb_ref[...])
    vs = [vn[t * n_seq:(t + 1) * n_seq] for t in range(seq_rows)]
    v_ref[...] = jnp.swapaxes(vn.reshape(seq_rows, n_seq, ds), 0, 1)

    gs = ds // N_SGU_GROUPS
    gate = []
    for t in range(seq_rows):
        parts = []
        for g in range(N_SGU_GROUPS):
            cols = slice(g * gs, (g + 1) * gs)
            mixed = bsp_ref[t:t + 1, cols]
            for s in range(t + 1):
                mixed = mixed + wsp_ref[(g * seq_rows + t) * seq_rows + s] * vs[s][:, cols]
            parts.append(mixed)
        gate.append((z_ref[t * n_seq:(t + 1) * n_seq, :ds] * jnp.concatenate(parts, axis=1)).astype(BF16))
    y = jnp.dot(jnp.concatenate(gate, axis=0), wout_ref[...], preferred_element_type=F32)
    x1 = [xs[t] + gt1 * y[t * n_seq:(t + 1) * n_seq] for t in range(seq_rows)]

    xn = jnp.concatenate([_modulate(x, ng_ref[1:2, :], sh2, sc2).astype(BF16) for x in x1], axis=0)
    y = _channel_mlp(xn, w1_ref, w2_ref, a_ref)
    out = [_rms_norm(x1[t] + gt2 * y[t * n_seq:(t + 1) * n_seq], fg_ref[...]) for t in range(seq_rows)]
    o_ref[...] = jnp.swapaxes(jnp.stack(out, axis=0), 0, 1)


def _layer1_weight_specs(layer, ng, win, b_in, lng, lnb, bias, wout, w1, w2, fg):
    return [_resident(ng.shape, layer), _resident(win.shape, 0), _resident(b_in.shape, 0),
            _resident(lng.shape, 0), _resident(lnb.shape, 0), _resident(bias.shape, 0),
            _resident(wout.shape, 0), _resident(w1.shape, layer), _resident(w2.shape, layer),
            _resident(fg.shape, 0)]


def _layer1_prompt(x, mod, n_lead, layer, ng, win, b_in, lng, lnb, wsp, bias, wout, w1, w2, fg):
    n_seq, seq, d = x.shape
    ds = wout.shape[1]
    d_ff = w1.shape[2]
    tm = TILE_ROWS
    v_rows = seq - CHUNK * ((seq - 1) // CHUNK)
    specs = _layer1_weight_specs(layer, ng, win, b_in, lng, lnb, bias, wout, w1, w2, fg)
    return pl.pallas_call(
        _layer1_prompt_kernel,
        grid=(n_seq, seq // tm),
        in_specs=[
            pl.BlockSpec((None, tm, d), lambda b, j: (b, j, 0)),
            pl.BlockSpec((None, n_seq, N_MOD * d), lambda b, j: (layer, n_lead // n_seq, 0)),
        ] + specs[:5] + [_resident(wsp.shape, 0)] + specs[5:],
        out_specs=[
            pl.BlockSpec((None, tm, d), lambda b, j: (b, j, 0)),
            pl.BlockSpec((None, v_rows, ds), lambda b, j: (b, 0, 0)),
        ],
        out_shape=[
            jax.ShapeDtypeStruct((n_seq, seq, d), F32),
            jax.ShapeDtypeStruct((n_seq, v_rows, ds), F32),
        ],
        scratch_shapes=[
            pltpu.VMEM((tm, 2 * ds), F32), pltpu.VMEM((tm, ds), BF16), pltpu.VMEM((tm, d_ff), BF16)],
        compiler_params=pltpu.CompilerParams(
            dimension_semantics=("arbitrary", "arbitrary"), vmem_limit_bytes=VMEM_LIMIT),
        name="layer1_prompt",
    )(x, mod, ng, win, b_in, lng, lnb, wsp, bias, wout, w1, w2, fg)


def _layer1_sample(x, mod, layer, wsp_corner, ng, win, b_in, lng, lnb, bias, wout, w1, w2, fg):
    n_seq, seq_rows, d = x.shape
    ds = wout.shape[1]
    d_ff = w1.shape[2]
    tb = SAMPLE_SEQS
    tm = tb * seq_rows
    return pl.pallas_call(
        functools.partial(_layer1_sample_kernel, seq_rows),
        grid=(n_seq // tb,),
        in_specs=[
            pl.BlockSpec(memory_space=pltpu.SMEM),
            pl.BlockSpec((tb, seq_rows, d), lambda i: (i, 0, 0)),
            pl.BlockSpec((None, tb, N_MOD * d), lambda i: (layer, i, 0)),
        ] + _layer1_weight_specs(layer, ng, win, b_in, lng, lnb, bias, wout, w1, w2, fg),
        out_specs=[pl.BlockSpec((tb, seq_rows, d), lambda i: (i, 0, 0)),
                   pl.BlockSpec((tb, seq_rows, ds), lambda i: (i, 0, 0))],
        out_shape=[jax.ShapeDtypeStruct(x.shape, F32), jax.ShapeDtypeStruct((n_seq, seq_rows, ds), F32)],
        scratch_shapes=[pltpu.VMEM((tm, 2 * ds), F32), pltpu.VMEM((tm, d_ff), BF16)],
        compiler_params=pltpu.CompilerParams(
            dimension_semantics=("arbitrary",), vmem_limit_bytes=VMEM_LIMIT),
        name="layer1_sample",
    )(wsp_corner, x, mod, ng, win, b_in, lng, lnb, bias, wout, w1, w2, fg)


def kernel(x_prompt, x_sample, c_prompt, c_sample, state_pool, norm_g, w_ada, b_ada, w_pool, pool_scale,
           sgu_w_in, sgu_b_in, sgu_ln_g, sgu_ln_b, sgu_w_sp, sgu_b_sp, sgu_w_out, mlp_w1, mlp_w2, final_g):
    n_p, seq, d = x_prompt.shape
    n_s, dec_seq, _ = x_sample.shape
    assert norm_g.shape[0] == 2 and state_pool.shape[0] == 1 and sgu_w_in.shape[0] == 1
    assert state_pool.shape[2] == POOL_BUF and dec_seq <= POOL_BUF and dec_seq <= CHUNK
    assert seq % TILE_ROWS == 0 and TILE_ROWS % CHUNK == 0 and TILE_ROWS >= HIST
    assert n_s % SAMPLE_SEQS == 0 and n_s % n_p == 0

    mod = _ada_modulation(jnp.concatenate([c_sample, c_prompt], axis=0), w_ada, b_ada)

    w1, w2 = mlp_w1.astype(BF16), mlp_w2.astype(BF16)
    l0 = (norm_g, w_pool.astype(BF16), pool_scale, w1, w2)
    xp, pool_p = _layer0_prompt(x_prompt, mod, n_s, 0, *l0)
    buf = jnp.transpose(state_pool[0], (1, 0, 2))
    xs, pool_s = _layer0_sample(x_sample, buf, mod, 0, *l0)

    ds = sgu_w_out.shape[1]
    bias = jnp.repeat(sgu_b_sp[0].T, ds // N_SGU_GROUPS, axis=1)[None]
    l1 = (norm_g, sgu_w_in.astype(BF16), sgu_b_in, sgu_ln_g, sgu_ln_b, bias, sgu_w_out.astype(BF16),
          w1, w2, final_g[None])
    yp, v_p = _layer1_prompt(xp, mod, n_s, 1, *l1[:5], sgu_w_sp, *l1[5:])
    corner = sgu_w_sp[0][:, :dec_seq, :dec_seq].reshape(-1)
    ys, v_s = _layer1_sample(xs, mod, 1, corner, *l1)

    return (yp, ys,
            jnp.transpose(pool_p, (1, 0, 2))[None], jnp.transpose(pool_s, (1, 0, 2))[None],
            v_p[None], v_s[None])
```

```python
import functools

import jax
import jax.numpy as jnp
from jax import lax
from jax.experimental import pallas as pl
from jax.experimental.pallas import tpu as pltpu

F32 = jnp.float32
BF16 = jnp.bfloat16

POOL_WINDOWS = (2, 4, 8, 16)
POOL_BUF = max(POOL_WINDOWS) - 1
HIST = 16
CHUNK = 128
N_SGU_GROUPS = 4
N_MOD = 6
EPS = 1e-6

TILE_ROWS = 512
SAMPLE_SEQS = 64
FF_COLS = 512
ADA_COLS = 2048
VMEM_LIMIT = 56 * 1024 * 1024


def _rms_norm(x, g):
    ms = jnp.mean(x * x, axis=-1, keepdims=True)
    return x * lax.rsqrt(ms + EPS) * g


def _modulate(x, g, shift, scale):
    return _rms_norm(x, g) * (1 + scale) + shift


def _gelu(z):
    return 0.5 * z * (1 + lax.erf(z * (0.5 ** 0.5)))


def _layer_norm(v, g, b):
    mu = jnp.mean(v, axis=-1, keepdims=True)
    vc = v - mu
    var = jnp.mean(vc * vc, axis=-1, keepdims=True)
    return vc * lax.rsqrt(var + EPS) * g + b


def _resident(shape, layer):
    if len(shape) == 2:
        assert shape[0] == 1 and layer == 0
        return pl.BlockSpec(tuple(shape), lambda *_: (0, 0), pipeline_mode=pl.Buffered(1))
    return pl.BlockSpec((None,) + tuple(shape[1:]), lambda *_: (layer,) + (0,) * (len(shape) - 1),
                        pipeline_mode=pl.Buffered(1))


def _split_mod(mod, d):
    return [mod[:, k * d:(k + 1) * d] for k in range(N_MOD)]


def _channel_mlp(xn, w1_ref, w2_ref, a_ref):
    d_ff = w1_ref.shape[1]
    for c in range(d_ff // FF_COLS):
        cols = slice(c * FF_COLS, (c + 1) * FF_COLS)
        a = jnp.maximum(jnp.dot(xn, w1_ref[:, cols], preferred_element_type=F32), 0.0)
        a_ref[:, cols] = (a * a).astype(BF16)
    return jnp.dot(a_ref[...], w2_ref[...], preferred_element_type=F32)


def _pace(lhs_ref, v):
    bits = pltpu.bitcast(v[0:8, 0:128].astype(F32), jnp.int32)
    zero = lax.shift_right_logical(lax.shift_right_logical(bits, 16), 16).astype(F32)
    rows = 32 // lhs_ref.dtype.itemsize
    zero = jnp.concatenate([zero] * (rows // 8), axis=0).astype(lhs_ref.dtype)
    lhs_ref[0:rows, 0:128] = lhs_ref[0:rows, 0:128] + zero


def _channel_mlp_with_side_work(xn_ref, w1_ref, w2_ref, a_ref, side_work):
    side = list(side_work)
    d_ff = w1_ref.shape[1]
    n_chunks = d_ff // FF_COLS
    assert len(side) <= n_chunks
    for c in range(n_chunks):
        cols = slice(c * FF_COLS, (c + 1) * FF_COLS)
        a = jnp.maximum(jnp.dot(xn_ref[...], w1_ref[:, cols], preferred_element_type=F32), 0.0)
        a_ref[:, cols] = (a * a).astype(BF16)
        if side:
            _pace(xn_ref if c + 1 < n_chunks else a_ref, side.pop(0)())
    return jnp.dot(a_ref[...], w2_ref[...], preferred_element_type=F32)


def _pool_project(delta, wp_ref, ps_ref):
    ys = [jnp.dot(dl, wp_ref[g], preferred_element_type=F32) for g, dl in enumerate(delta)]
    return jnp.concatenate(ys, axis=1) * ps_ref[...]


def _sgu_in(h, win_ref, bin_ref, z_ref):
    for c in range(win_ref.shape[1] // FF_COLS):
        cols = slice(c * FF_COLS, (c + 1) * FF_COLS)
        z = jnp.dot(h, win_ref[:, cols], preferred_element_type=F32) + bin_ref[:, cols]
        z_ref[:, cols] = _gelu(z)


def _ada_kernel(c_ref, w_ref, b_ref, o_ref):
    s = jax.nn.silu(c_ref[...]).astype(BF16)
    b = b_ref[pl.ds(pl.program_id(0), 1), :]
    o_ref[...] = jnp.dot(s, w_ref[...].astype(BF16), preferred_element_type=F32) + b


def _ada_modulation(c_all, w_ada, b_ada):
    depth, d, n = w_ada.shape
    rows = c_all.shape[0]
    return pl.pallas_call(
        _ada_kernel,
        grid=(depth, n // ADA_COLS),
        in_specs=[
            pl.BlockSpec((rows, d), lambda l, j: (0, 0)),
            pl.BlockSpec((None, d, ADA_COLS), lambda l, j: (l, 0, j)),
            pl.BlockSpec((depth, ADA_COLS), lambda l, j: (0, j)),
        ],
        out_specs=pl.BlockSpec((None, rows, ADA_COLS), lambda l, j: (l, 0, j)),
        out_shape=jax.ShapeDtypeStruct((depth, rows, n), F32),
        compiler_params=pltpu.CompilerParams(
            dimension_semantics=("arbitrary", "arbitrary"), vmem_limit_bytes=VMEM_LIMIT),
        name="ada_modulation",
    )(c_all, w_ada, b_ada)


def _pipelined_steps(init, mixer_pieces, mlp):
    s = pl.program_id(0)
    n_tiles = pl.num_programs(0) - 1

    @pl.when(s == 0)
    def _():
        init()
        for piece in mixer_pieces(s):
            piece()

    @pl.when(s > 0)
    def _():
        mlp(s - 1, mixer_pieces(jnp.minimum(s, n_tiles - 1)))


def _handoff(xn_next, x1_next, xn_cur, x1_cur):
    xn_cur[...] = xn_next[...]
    x1_cur[...] = x1_next[...]


def _pipelined_tile_maps(n_seq, tiles_per_seq):
    n_tiles = n_seq * tiles_per_seq

    def mixer_tile(s):
        t = jnp.minimum(s, n_tiles - 1)
        return t // tiles_per_seq, t % tiles_per_seq, 0

    def mlp_tile(s):
        t = jnp.maximum(s - 1, 0)
        return t // tiles_per_seq, t % tiles_per_seq, 0

    return mixer_tile, mlp_tile, n_tiles + 1


def _pipeline_scratch(tm, d, d_ff):
    return [pltpu.VMEM((tm, d), BF16), pltpu.VMEM((tm, d), F32), pltpu.VMEM((tm, d), BF16),
            pltpu.VMEM((tm, d), F32), pltpu.VMEM((tm, d_ff), BF16)]


def _layer0_prompt_kernel(tiles_per_seq, x_ref, mod_ref, ng_ref, wp_ref, ps_ref, w1_ref, w2_ref,
                          o_ref, tail_ref, hh_ref, xn_next, x1_next, xn_cur, x1_cur, a_ref):
    tm, d = x_ref.shape
    pg = d // len(POOL_WINDOWS)

    def mixer_pieces(tile):
        b, j = tile // tiles_per_seq, tile % tiles_per_seq
        sh1, sc1, gt1, sh2, sc2, _ = _split_mod(mod_ref[pl.ds(b, 1), :], d)
        first = j == 0

        def norm():
            h = _modulate(x_ref[...], ng_ref[0:1, :], sh1, sc1)
            hh_ref[0:HIST] = jnp.where(first, 0.0, hh_ref[tm:tm + HIST])
            hh_ref[HIST:] = h
            for k in range(POOL_BUF):
                tail_ref[k, pl.ds(b, 1), :] = h[tm - POOL_BUF + k:tm - POOL_BUF + k + 1]
            return h

        def group(g):
            w = POOL_WINDOWS[g]
            cols = slice(g * pg, (g + 1) * pg)
            s = hh_ref[:, cols]
            span = 1
            while span < w:
                s = s + pltpu.roll(s, span, axis=0)
                span *= 2
            s = s[HIST:]
            r = lax.broadcasted_iota(jnp.int32, (HIST, pg), 0)
            cnt = jnp.where(first, jnp.minimum(w, r + 1), w).astype(F32)
            m = jnp.concatenate([s[:HIST] / cnt, s[HIST:] * (1.0 / w)], axis=0)
            delta = (m - hh_ref[HIST:, cols]).astype(BF16)
            y = jnp.dot(delta, wp_ref[g], preferred_element_type=F32) * ps_ref[:, cols]
            x1 = x_ref[:, cols] + gt1[:, cols] * y
            x1_next[:, cols] = x1
            return x1

        def norm2():
            xn = _modulate(x1_next[...], ng_ref[1:2, :], sh2, sc2)
            xn_next[...] = xn.astype(BF16)
            return xn

        return [norm] + [functools.partial(group, g) for g in range(len(POOL_WINDOWS))] + [norm2]

    def mlp(tile, side_work):
        gt2 = mod_ref[pl.ds(tile // tiles_per_seq, 1), (N_MOD - 1) * d:]
        _handoff(xn_next, x1_next, xn_cur, x1_cur)
        y = _channel_mlp_with_side_work(xn_cur, w1_ref, w2_ref, a_ref, side_work)
        o_ref[...] = x1_cur[...] + gt2 * y

    def init():
        hh_ref[...] = jnp.zeros_like(hh_ref)

    _pipelined_steps(init, mixer_pieces, mlp)


def _layer0_sample_kernel(seq_rows, x_ref, buf_ref, mod_ref, ng_ref, wp_ref, ps_ref, w1_ref, w2_ref,
                          o_ref, pool_ref, a_ref):
    n_seq, _, d = x_ref.shape
    sh1, sc1, gt1, sh2, sc2, gt2 = _split_mod(mod_ref[...], d)

    xt = jnp.swapaxes(x_ref[...], 0, 1)
    xs = [xt[t] for t in range(seq_rows)]
    hs = [_modulate(x, ng_ref[0:1, :], sh1, sc1) for x in xs]
    rows = [buf_ref[k] for k in range(POOL_BUF)] + hs
    for k in range(POOL_BUF):
        pool_ref[k] = rows[len(rows) - POOL_BUF + k]

    pg = d // len(POOL_WINDOWS)
    delta = []
    for g, w in enumerate(POOL_WINDOWS):
        cols = slice(g * pg, (g + 1) * pg)
        sums = {i: rows[i][:, cols] for i in range(len(rows))}
        span = 1
        while span < w:
            first = POOL_BUF - (w - 2 * span)
            sums = {i: sums[i] + sums[i - span] for i in range(first, len(rows))}
            span *= 2
        delta.append(jnp.concatenate(
            [(sums[POOL_BUF + t] * (1.0 / w) - hs[t][:, cols]).astype(BF16) for t in range(seq_rows)], axis=0))
    y = _pool_project(delta, wp_ref, ps_ref)
    x1 = [xs[t] + gt1 * y[t * n_seq:(t + 1) * n_seq] for t in range(seq_rows)]

    xn = jnp.concatenate([_modulate(x, ng_ref[1:2, :], sh2, sc2).astype(BF16) for x in x1], axis=0)
    y = _channel_mlp(xn, w1_ref, w2_ref, a_ref)
    out = [x1[t] + gt2 * y[t * n_seq:(t + 1) * n_seq] for t in range(seq_rows)]
    o_ref[...] = jnp.swapaxes(jnp.stack(out, axis=0), 0, 1)


def _layer0_prompt(x, mod, n_lead, layer, ng, wp, ps, w1, w2):
    n_seq, seq, d = x.shape
    d_ff = w1.shape[2]
    tm = TILE_ROWS
    mixer_tile, mlp_tile, n_steps = _pipelined_tile_maps(n_seq, seq // tm)
    return pl.pallas_call(
        functools.partial(_layer0_prompt_kernel, seq // tm),
        grid=(n_steps,),
        in_specs=[
            pl.BlockSpec((None, tm, d), mixer_tile),
            pl.BlockSpec((None, n_seq, N_MOD * d), lambda s: (layer, n_lead // n_seq, 0)),
            _resident(ng.shape, layer), _resident(wp.shape, 0), _resident(ps.shape, 0),
            _resident(w1.shape, layer), _resident(w2.shape, layer),
        ],
        out_specs=[
            pl.BlockSpec((None, tm, d), mlp_tile),
            pl.BlockSpec((POOL_BUF, n_seq, d), lambda s: (0, 0, 0)),
        ],
        out_shape=[
            jax.ShapeDtypeStruct((n_seq, seq, d), F32),
            jax.ShapeDtypeStruct((POOL_BUF, n_seq, d), F32),
        ],
        scratch_shapes=[pltpu.VMEM((HIST + tm, d), F32)] + _pipeline_scratch(tm, d, d_ff),
        compiler_params=pltpu.CompilerParams(
            dimension_semantics=("arbitrary",), vmem_limit_bytes=VMEM_LIMIT),
        name="layer0_prompt",
    )(x, mod, ng, wp, ps, w1, w2)


def _layer0_sample(x, buf, mod, layer, ng, wp, ps, w1, w2):
    n_seq, seq_rows, d = x.shape
    d_ff = w1.shape[2]
    tb = SAMPLE_SEQS
    tm = tb * seq_rows
    return pl.pallas_call(
        functools.partial(_layer0_sample_kernel, seq_rows),
        grid=(n_seq // tb,),
        in_specs=[
            pl.BlockSpec((tb, seq_rows, d), lambda i: (i, 0, 0)),
            pl.BlockSpec((POOL_BUF, tb, d), lambda i: (0, i, 0)),
            pl.BlockSpec((None, tb, N_MOD * d), lambda i: (layer, i, 0)),
            _resident(ng.shape, layer), _resident(wp.shape, 0), _resident(ps.shape, 0),
            _resident(w1.shape, layer), _resident(w2.shape, layer),
        ],
        out_specs=[pl.BlockSpec((tb, seq_rows, d), lambda i: (i, 0, 0)),
                   pl.BlockSpec((POOL_BUF, tb, d), lambda i: (0, i, 0))],
        out_shape=[jax.ShapeDtypeStruct(x.shape, F32), jax.ShapeDtypeStruct(buf.shape, F32)],
        scratch_shapes=[pltpu.VMEM((tm, d_ff), BF16)],
        compiler_params=pltpu.CompilerParams(
            dimension_semantics=("arbitrary",), vmem_limit_bytes=VMEM_LIMIT),
        name="layer0_sample",
    )(x, buf, mod, ng, wp, ps, w1, w2)


def _layer1_prompt_kernel(tiles_per_seq, x_ref, mod_ref, ng_ref, win_ref, bin_ref, lng_ref, lnb_ref,
                          wsp_ref, bsp_ref, wout_ref, w1_ref, w2_ref, fg_ref, o_ref, v_ref,
                          hb_ref, z_ref, vnb_ref, gate_ref, xn_next, x1_next, xn_cur, x1_cur, a_ref):
    tm, d = x_ref.shape
    ds = wout_ref.shape[0]
    v_rows = v_ref.shape[0]
    gs = ds // N_SGU_GROUPS

    def mixer_pieces(tile):
        sh1, sc1, gt1, sh2, sc2, _ = _split_mod(mod_ref[pl.ds(tile // tiles_per_seq, 1), :], d)

        def norm():
            h = _modulate(x_ref[...], ng_ref[0:1, :], sh1, sc1)
            hb_ref[...] = h.astype(BF16)
            return h

        def sgu_in(c):
            cols = slice(c * FF_COLS, (c + 1) * FF_COLS)
            z = jnp.dot(hb_ref[...], win_ref[:, cols], preferred_element_type=F32) + bin_ref[:, cols]
            z = _gelu(z)
            z_ref[:, cols] = z
            return z

        def norm_v():
            vn = _layer_norm(z_ref[:, ds:], lng_ref[...], lnb_ref[...])
            v_ref[...] = vn[tm - v_rows:]
            vnb_ref[...] = vn.astype(BF16)
            return vn

        def spatial():
            r = lax.broadcasted_iota(jnp.int32, (CHUNK, CHUNK), 0)
            c = lax.broadcasted_iota(jnp.int32, (CHUNK, CHUNK), 1)
            for g in range(N_SGU_GROUPS):
                wt = jnp.where(c <= r, wsp_ref[g], 0.0).astype(BF16)
                cols = slice(g * gs, (g + 1) * gs)
                for ci in range(tm // CHUNK):
                    rows = slice(ci * CHUNK, (ci + 1) * CHUNK)
                    mixed = jnp.dot(wt, vnb_ref[rows, cols], preferred_element_type=F32) + bsp_ref[:, cols]
                    gate_ref[rows, cols] = (z_ref[rows, cols] * mixed).astype(BF16)
            return mixed

        def sgu_out():
            x1 = x_ref[...] + gt1 * jnp.dot(gate_ref[...], wout_ref[...], preferred_element_type=F32)
            x1_next[...] = x1
            xn = _modulate(x1, ng_ref[1:2, :], sh2, sc2)
            xn_next[...] = xn.astype(BF16)
            return xn

        return ([norm] + [functools.partial(sgu_in, c) for c in range(2 * ds // FF_COLS)]
                + [norm_v, spatial, sgu_out])

    def mlp(tile, side_work):
        gt2 = mod_ref[pl.ds(tile // tiles_per_seq, 1), (N_MOD - 1) * d:]
        _handoff(xn_next, x1_next, xn_cur, x1_cur)
        x2 = x1_cur[...] + gt2 * _channel_mlp_with_side_work(xn_cur, w1_ref, w2_ref, a_ref, side_work)
        o_ref[...] = _rms_norm(x2, fg_ref[...])

    _pipelined_steps(lambda: None, mixer_pieces, mlp)


def _layer1_sample_kernel(seq_rows, wsp_ref, x_ref, mod_ref, ng_ref, win_ref, bin_ref, lng_ref, lnb_ref,
                          bsp_ref, wout_ref, w1_ref, w2_ref, fg_ref, o_ref, v_ref, z_ref, a_ref):
    n_seq, _, d = x_ref.shape
    ds = wout_ref.shape[0]
    sh1, sc1, gt1, sh2, sc2, gt2 = _split_mod(mod_ref[...], d)

    xt = jnp.swapaxes(x_ref[...], 0, 1)
    xs = [xt[t] for t in range(seq_rows)]
    h = jnp.concatenate([_modulate(x, ng_ref[0:1, :], sh1, sc1).astype(BF16) for x in xs], axis=0)
    _sgu_in(h, win_ref, bin_ref, z_ref)
    vn = _layer_norm(z_ref[:, ds:], lng_ref[...], lnb_ref[...])
    vs = [vn[t * n_seq:(t + 1) * n_seq] for t in range(seq_rows)]
    v_ref[...] = jnp.swapaxes(vn.reshape(seq_rows, n_seq, ds), 0, 1)

    gs = ds // N_SGU_GROUPS
    gate = []
    for t in range(seq_rows):
        parts = []
        for g in range(N_SGU_GROUPS):
            cols = slice(g * gs, (g + 1) * gs)
            mixed = bsp_ref[t:t + 1, cols]
            for s in range(t + 1):
                mixed = mixed + wsp_ref[(g * seq_rows + t) * seq_rows + s] * vs[s][:, cols]
            parts.append(mixed)
        gate.append((z_ref[t * n_seq:(t + 1) * n_seq, :ds] * jnp.concatenate(parts, axis=1)).astype(BF16))
    y = jnp.dot(jnp.concatenate(gate, axis=0), wout_ref[...], preferred_element_type=F32)
    x1 = [xs[t] + gt1 * y[t * n_seq:(t + 1) * n_seq] for t in range(seq_rows)]

    xn = jnp.concatenate([_modulate(x, ng_ref[1:2, :], sh2, sc2).astype(BF16) for x in x1], axis=0)
    y = _channel_mlp(xn, w1_ref, w2_ref, a_ref)
    out = [_rms_norm(x1[t] + gt2 * y[t * n_seq:(t + 1) * n_seq], fg_ref[...]) for t in range(seq_rows)]
    o_ref[...] = jnp.swapaxes(jnp.stack(out, axis=0), 0, 1)


def _layer1_weight_specs(layer, ng, win, b_in, lng, lnb, bias, wout, w1, w2, fg):
    return [_resident(ng.shape, layer), _resident(win.shape, 0), _resident(b_in.shape, 0),
            _resident(lng.shape, 0), _resident(lnb.shape, 0), _resident(bias.shape, 0),
            _resident(wout.shape, 0), _resident(w1.shape, layer), _resident(w2.shape, layer),
            _resident(fg.shape, 0)]


def _layer1_prompt(x, mod, n_lead, layer, ng, win, b_in, lng, lnb, wsp, bias, wout, w1, w2, fg):
    n_seq, seq, d = x.shape
    ds = wout.shape[1]
    d_ff = w1.shape[2]
    tm = TILE_ROWS
    v_rows = seq - CHUNK * ((seq - 1) // CHUNK)
    specs = _layer1_weight_specs(layer, ng, win, b_in, lng, lnb, bias, wout, w1, w2, fg)
    mixer_tile, mlp_tile, n_steps = _pipelined_tile_maps(n_seq, seq // tm)
    return pl.pallas_call(
        functools.partial(_layer1_prompt_kernel, seq // tm),
        grid=(n_steps,),
        in_specs=[
            pl.BlockSpec((None, tm, d), mixer_tile),
            pl.BlockSpec((None, n_seq, N_MOD * d), lambda s: (layer, n_lead // n_seq, 0)),
        ] + specs[:5] + [_resident(wsp.shape, 0)] + specs[5:],
        out_specs=[
            pl.BlockSpec((None, tm, d), mlp_tile),
            pl.BlockSpec((None, v_rows, ds), lambda s: (mixer_tile(s)[0], 0, 0)),
        ],
        out_shape=[
            jax.ShapeDtypeStruct((n_seq, seq, d), F32),
            jax.ShapeDtypeStruct((n_seq, v_rows, ds), F32),
        ],
        scratch_shapes=[pltpu.VMEM((tm, d), BF16), pltpu.VMEM((tm, 2 * ds), F32), pltpu.VMEM((tm, ds), BF16),
                        pltpu.VMEM((tm, ds), BF16)] + _pipeline_scratch(tm, d, d_ff),
        compiler_params=pltpu.CompilerParams(
            dimension_semantics=("arbitrary",), vmem_limit_bytes=VMEM_LIMIT),
        name="layer1_prompt",
    )(x, mod, ng, win, b_in, lng, lnb, wsp, bias, wout, w1, w2, fg)


def _layer1_sample(x, mod, layer, wsp_corner, ng, win, b_in, lng, lnb, bias, wout, w1, w2, fg):
    n_seq, seq_rows, d = x.shape
    ds = wout.shape[1]
    d_ff = w1.shape[2]
    tb = SAMPLE_SEQS
    tm = tb * seq_rows
    return pl.pallas_call(
        functools.partial(_layer1_sample_kernel, seq_rows),
        grid=(n_seq // tb,),
        in_specs=[
            pl.BlockSpec(memory_space=pltpu.SMEM),
            pl.BlockSpec((tb, seq_rows, d), lambda i: (i, 0, 0)),
            pl.BlockSpec((None, tb, N_MOD * d), lambda i: (layer, i, 0)),
        ] + _layer1_weight_specs(layer, ng, win, b_in, lng, lnb, bias, wout, w1, w2, fg),
        out_specs=[pl.BlockSpec((tb, seq_rows, d), lambda i: (i, 0, 0)),
                   pl.BlockSpec((tb, seq_rows, ds), lambda i: (i, 0, 0))],
        out_shape=[jax.ShapeDtypeStruct(x.shape, F32), jax.ShapeDtypeStruct((n_seq, seq_rows, ds), F32)],
        scratch_shapes=[pltpu.VMEM((tm, 2 * ds), F32), pltpu.VMEM((tm, d_ff), BF16)],
        compiler_params=pltpu.CompilerParams(
            dimension_semantics=("arbitrary",), vmem_limit_bytes=VMEM_LIMIT),
        name="layer1_sample",
    )(wsp_corner, x, mod, ng, win, b_in, lng, lnb, bias, wout, w1, w2, fg)


def kernel(x_prompt, x_sample, c_prompt, c_sample, state_pool, norm_g, w_ada, b_ada, w_pool, pool_scale,
           sgu_w_in, sgu_b_in, sgu_ln_g, sgu_ln_b, sgu_w_sp, sgu_b_sp, sgu_w_out, mlp_w1, mlp_w2, final_g):
    n_p, seq, d = x_prompt.shape
    n_s, dec_seq, _ = x_sample.shape
    assert norm_g.shape[0] == 2 and state_pool.shape[0] == 1 and sgu_w_in.shape[0] == 1
    assert state_pool.shape[2] == POOL_BUF and dec_seq <= POOL_BUF and dec_seq <= CHUNK
    assert seq % TILE_ROWS == 0 and TILE_ROWS % CHUNK == 0 and TILE_ROWS >= HIST
    assert n_s % SAMPLE_SEQS == 0 and n_s % n_p == 0

    mod = _ada_modulation(jnp.concatenate([c_sample, c_prompt], axis=0), w_ada, b_ada)

    w1, w2 = mlp_w1.astype(BF16), mlp_w2.astype(BF16)
    l0 = (norm_g, w_pool.astype(BF16), pool_scale, w1, w2)
    xp, pool_p = _layer0_prompt(x_prompt, mod, n_s, 0, *l0)
    buf = jnp.transpose(state_pool[0], (1, 0, 2))
    xs, pool_s = _layer0_sample(x_sample, buf, mod, 0, *l0)

    ds = sgu_w_out.shape[1]
    bias = jnp.repeat(sgu_b_sp[0].T, ds // N_SGU_GROUPS, axis=1)[None]
    l1 = (norm_g, sgu_w_in.astype(BF16), sgu_b_in, sgu_ln_g, sgu_ln_b, bias, sgu_w_out.astype(BF16),
          w1, w2, final_g[None])
    yp, v_p = _layer1_prompt(xp, mod, n_s, 1, *l1[:5], sgu_w_sp, *l1[5:])
    corner = sgu_w_sp[0][:, :dec_seq, :dec_seq].reshape(-1)
    ys, v_s = _layer1_sample(xs, mod, 1, corner, *l1)

    return (yp, ys,
            jnp.transpose(pool_p, (1, 0, 2))[None], jnp.transpose(pool_s, (1, 0, 2))[None],
            v_p[None], v_s[None])
```

```python
import functools

import jax
import jax.numpy as jnp
from jax import lax
from jax.experimental import pallas as pl
from jax.experimental.pallas import tpu as pltpu

F32 = jnp.float32
BF16 = jnp.bfloat16

POOL_WINDOWS = (2, 4, 8, 16)
POOL_BUF = max(POOL_WINDOWS) - 1
HIST = 16
CHUNK = 128
N_SGU_GROUPS = 4
N_MOD = 6
EPS = 1e-6

TILE_ROWS = 512
SAMPLE_SEQS = 64
FF_COLS = 512
ADA_COLS = 2048
ADA_PREP_COLS = 256
VMEM_LIMIT = 56 * 1024 * 1024


def _rms_norm(x, g):
    ms = jnp.mean(x * x, axis=-1, keepdims=True)
    return x * lax.rsqrt(ms + EPS) * g


def _modulate(x, g, shift, scale):
    return _rms_norm(x, g) * (1 + scale) + shift


def _gelu(z):
    return 0.5 * z * (1 + lax.erf(z * (0.5 ** 0.5)))


def _layer_norm(v, g, b):
    mu = jnp.mean(v, axis=-1, keepdims=True)
    vc = v - mu
    var = jnp.mean(vc * vc, axis=-1, keepdims=True)
    return vc * lax.rsqrt(var + EPS) * g + b


def _resident(shape, layer):
    if len(shape) == 2:
        assert shape[0] == 1 and layer == 0
        return pl.BlockSpec(tuple(shape), lambda *_: (0, 0), pipeline_mode=pl.Buffered(1))
    return pl.BlockSpec((None,) + tuple(shape[1:]), lambda *_: (layer,) + (0,) * (len(shape) - 1),
                        pipeline_mode=pl.Buffered(1))


def _split_mod(mod, d):
    return [mod[:, k * d:(k + 1) * d] for k in range(N_MOD)]


def _channel_mlp(xn, w1_ref, w2_ref, a_ref):
    d_ff = w1_ref.shape[1]
    for c in range(d_ff // FF_COLS):
        cols = slice(c * FF_COLS, (c + 1) * FF_COLS)
        a = jnp.maximum(jnp.dot(xn, w1_ref[:, cols], preferred_element_type=F32), 0.0)
        a_ref[:, cols] = (a * a).astype(BF16)
    return jnp.dot(a_ref[...], w2_ref[...], preferred_element_type=F32)


def _pace(lhs_ref, v):
    bits = pltpu.bitcast(v[0:8, 0:128].astype(F32), jnp.int32)
    zero = lax.shift_right_logical(lax.shift_right_logical(bits, 16), 16).astype(F32)
    rows = 32 // lhs_ref.dtype.itemsize
    zero = jnp.concatenate([zero] * (rows // 8), axis=0).astype(lhs_ref.dtype)
    lhs_ref[0:rows, 0:128] = lhs_ref[0:rows, 0:128] + zero


def _channel_mlp_with_side_work(xn_ref, w1_ref, w2_ref, a_ref, side_work):
    side = list(side_work)
    d_ff = w1_ref.shape[1]
    n_chunks = d_ff // FF_COLS
    assert len(side) <= n_chunks
    for c in range(n_chunks):
        cols = slice(c * FF_COLS, (c + 1) * FF_COLS)
        a = jnp.maximum(jnp.dot(xn_ref[...], w1_ref[:, cols], preferred_element_type=F32), 0.0)
        a_ref[:, cols] = (a * a).astype(BF16)
        if side:
            _pace(xn_ref if c + 1 < n_chunks else a_ref, side.pop(0)())
    return jnp.dot(a_ref[...], w2_ref[...], preferred_element_type=F32)


def _pool_project(delta, wp_ref, ps_ref):
    ys = [jnp.dot(dl, wp_ref[g], preferred_element_type=F32) for g, dl in enumerate(delta)]
    return jnp.concatenate(ys, axis=1) * ps_ref[...]


def _sgu_in(h, win_ref, bin_ref, z_ref):
    for c in range(win_ref.shape[1] // FF_COLS):
        cols = slice(c * FF_COLS, (c + 1) * FF_COLS)
        z = jnp.dot(h, win_ref[:, cols], preferred_element_type=F32) + bin_ref[:, cols]
        z_ref[:, cols] = _gelu(z)


def _ada_rows(c_ref, w_ref, b_row):
    s = jax.nn.silu(c_ref[...]).astype(BF16)
    return jnp.dot(s, w_ref[...].astype(BF16), preferred_element_type=F32) + b_row


def _ada_kernel(layer, c_ref, w_ref, b_ref, o_ref):
    o_ref[...] = _ada_rows(c_ref, w_ref, b_ref[layer:layer + 1, :])


def _ada_modulation(c_all, w_ada, b_ada, layer):
    depth, d, n = w_ada.shape
    rows = c_all.shape[0]
    return pl.pallas_call(
        functools.partial(_ada_kernel, layer),
        grid=(n // ADA_COLS,),
        in_specs=[
            pl.BlockSpec((rows, d), lambda j: (0, 0)),
            pl.BlockSpec((None, d, ADA_COLS), lambda j: (layer, 0, j)),
            pl.BlockSpec((depth, ADA_COLS), lambda j: (0, j)),
        ],
        out_specs=pl.BlockSpec((rows, ADA_COLS), lambda j: (0, j)),
        out_shape=jax.ShapeDtypeStruct((rows, n), F32),
        compiler_params=pltpu.CompilerParams(
            dimension_semantics=("arbitrary",), vmem_limit_bytes=VMEM_LIMIT),
        name="ada_modulation",
    )(c_all, w_ada, b_ada)


def _pipelined_steps(init, mixer_pieces, mlp):
    s = pl.program_id(0)
    n_tiles = pl.num_programs(0) - 1

    @pl.when(s == 0)
    def _():
        init()
        for piece in mixer_pieces(s):
            piece()

    @pl.when(s > 0)
    def _():
        mlp(s - 1, mixer_pieces(jnp.minimum(s, n_tiles - 1)))


def _handoff(xn_next, x1_next, xn_cur, x1_cur):
    xn_cur[...] = xn_next[...]
    x1_cur[...] = x1_next[...]


def _pipelined_tile_maps(n_seq, tiles_per_seq):
    n_tiles = n_seq * tiles_per_seq

    def mixer_tile(s):
        t = jnp.minimum(s, n_tiles - 1)
        return t // tiles_per_seq, t % tiles_per_seq, 0

    def mlp_tile(s):
        t = jnp.maximum(s - 1, 0)
        return t // tiles_per_seq, t % tiles_per_seq, 0

    return mixer_tile, mlp_tile, n_tiles + 1


def _pipeline_scratch(tm, d, d_ff):
    return [pltpu.VMEM((tm, d), BF16), pltpu.VMEM((tm, d), F32), pltpu.VMEM((tm, d), BF16),
            pltpu.VMEM((tm, d), F32), pltpu.VMEM((tm, d_ff), BF16)]


def _layer0_prompt_kernel(tiles_per_seq, next_layer, x_ref, mod_ref, ng_ref, wp_ref, ps_ref, w1_ref, w2_ref,
                          c_ref, wada_ref, bada_ref, *rest):
    n_cast = (len(rest) - 9) // 2
    cast_src = rest[:n_cast]
    o_ref, tail_ref, mod_next_ref = rest[n_cast:n_cast + 3]
    cast_dst = rest[n_cast + 3:2 * n_cast + 3]
    hh_ref, xn_next, x1_next, xn_cur, x1_cur, a_ref = rest[2 * n_cast + 3:]
    tm, d = x_ref.shape
    pg = d // len(POOL_WINDOWS)

    def next_layer_prep():
        mod_next = _ada_rows(c_ref, wada_ref, bada_ref[next_layer:next_layer + 1, :])
        mod_next_ref[...] = mod_next
        for src, dst in zip(cast_src, cast_dst):
            dst[...] = src[...].astype(BF16)
        return mod_next

    def mixer_pieces(tile):
        b, j = tile // tiles_per_seq, tile % tiles_per_seq
        sh1, sc1, gt1, sh2, sc2, _ = _split_mod(mod_ref[pl.ds(b, 1), :], d)
        first = j == 0

        def norm():
            h = _modulate(x_ref[...], ng_ref[0:1, :], sh1, sc1)
            hh_ref[0:HIST] = jnp.where(first, 0.0, hh_ref[tm:tm + HIST])
            hh_ref[HIST:] = h
            for k in range(POOL_BUF):
                tail_ref[k, pl.ds(b, 1), :] = h[tm - POOL_BUF + k:tm - POOL_BUF + k + 1]
            return h

        def group(g):
            w = POOL_WINDOWS[g]
            cols = slice(g * pg, (g + 1) * pg)
            s = hh_ref[:, cols]
            span = 1
            while span < w:
                s = s + pltpu.roll(s, span, axis=0)
                span *= 2
            s = s[HIST:]
            r = lax.broadcasted_iota(jnp.int32, (HIST, pg), 0)
            cnt = jnp.where(first, jnp.minimum(w, r + 1), w).astype(F32)
            m = jnp.concatenate([s[:HIST] / cnt, s[HIST:] * (1.0 / w)], axis=0)
            delta = (m - hh_ref[HIST:, cols]).astype(BF16)
            y = jnp.dot(delta, wp_ref[g], preferred_element_type=F32) * ps_ref[:, cols]
            x1 = x_ref[:, cols] + gt1[:, cols] * y
            x1_next[:, cols] = x1
            return x1

        def norm2():
            xn = _modulate(x1_next[...], ng_ref[1:2, :], sh2, sc2)
            xn_next[...] = xn.astype(BF16)
            return xn

        return ([norm] + [functools.partial(group, g) for g in range(len(POOL_WINDOWS))]
                + [norm2, next_layer_prep])

    def mlp(tile, side_work):
        gt2 = mod_ref[pl.ds(tile // tiles_per_seq, 1), (N_MOD - 1) * d:]
        _handoff(xn_next, x1_next, xn_cur, x1_cur)
        y = _channel_mlp_with_side_work(xn_cur, w1_ref, w2_ref, a_ref, side_work)
        o_ref[...] = x1_cur[...] + gt2 * y

    def init():
        hh_ref[...] = jnp.zeros_like(hh_ref)

    _pipelined_steps(init, mixer_pieces, mlp)


def _layer0_sample_kernel(seq_rows, x_ref, buf_ref, mod_ref, ng_ref, wp_ref, ps_ref, w1_ref, w2_ref,
                          o_ref, pool_ref, a_ref):
    n_seq, _, d = x_ref.shape
    sh1, sc1, gt1, sh2, sc2, gt2 = _split_mod(mod_ref[...], d)

    xt = jnp.swapaxes(x_ref[...], 0, 1)
    xs = [xt[t] for t in range(seq_rows)]
    hs = [_modulate(x, ng_ref[0:1, :], sh1, sc1) for x in xs]
    rows = [buf_ref[k] for k in range(POOL_BUF)] + hs
    for k in range(POOL_BUF):
        pool_ref[k] = rows[len(rows) - POOL_BUF + k]

    pg = d // len(POOL_WINDOWS)
    delta = []
    for g, w in enumerate(POOL_WINDOWS):
        cols = slice(g * pg, (g + 1) * pg)
        sums = {i: rows[i][:, cols] for i in range(len(rows))}
        span = 1
        while span < w:
            first = POOL_BUF - (w - 2 * span)
            sums = {i: sums[i] + sums[i - span] for i in range(first, len(rows))}
            span *= 2
        delta.append(jnp.concatenate(
            [(sums[POOL_BUF + t] * (1.0 / w) - hs[t][:, cols]).astype(BF16) for t in range(seq_rows)], axis=0))
    y = _pool_project(delta, wp_ref, ps_ref)
    x1 = [xs[t] + gt1 * y[t * n_seq:(t + 1) * n_seq] for t in range(seq_rows)]

    xn = jnp.concatenate([_modulate(x, ng_ref[1:2, :], sh2, sc2).astype(BF16) for x in x1], axis=0)
    y = _channel_mlp(xn, w1_ref, w2_ref, a_ref)
    out = [x1[t] + gt2 * y[t * n_seq:(t + 1) * n_seq] for t in range(seq_rows)]
    o_ref[...] = jnp.swapaxes(jnp.stack(out, axis=0), 0, 1)


def _layer0_prompt(x, mod, n_lead, layer, ng, wp, ps, w1, w2, c_all, w_ada, b_ada, next_weights):
    n_seq, seq, d = x.shape
    d_ff = w1.shape[2]
    tm = TILE_ROWS
    mixer_tile, mlp_tile, n_steps = _pipelined_tile_maps(n_seq, seq // tm)
    n_tiles = n_steps - 1
    n_mod = w_ada.shape[2]
    assert n_mod % ADA_PREP_COLS == 0 and n_mod // ADA_PREP_COLS <= n_tiles
    ada_block = lambda s: jnp.minimum(s, n_mod // ADA_PREP_COLS - 1)
    cast_in, cast_out, cast_shapes = [], [], []
    for w, l in next_weights:
        rows = w.shape[1] // n_tiles
        assert w.shape[1] % n_tiles == 0 and rows % 16 == 0
        cast_in.append(pl.BlockSpec((None, rows, w.shape[2]),
                                    lambda s, l=l: (l, jnp.minimum(s, n_tiles - 1), 0)))
        cast_out.append(pl.BlockSpec((None, rows, w.shape[2]), lambda s: (0, jnp.minimum(s, n_tiles - 1), 0)))
        cast_shapes.append(jax.ShapeDtypeStruct((1,) + w.shape[1:], BF16))
    return pl.pallas_call(
        functools.partial(_layer0_prompt_kernel, seq // tm, layer + 1),
        grid=(n_steps,),
        in_specs=[
            pl.BlockSpec((None, tm, d), mixer_tile),
            pl.BlockSpec((n_seq, N_MOD * d), lambda s: (n_lead // n_seq, 0)),
            _resident(ng.shape, layer), _resident(wp.shape, 0), _resident(ps.shape, 0),
            _resident(w1.shape, 0), _resident(w2.shape, 0),
            pl.BlockSpec(c_all.shape, lambda s: (0, 0)),
            pl.BlockSpec((None, d, ADA_PREP_COLS), lambda s: (layer + 1, 0, ada_block(s))),
            pl.BlockSpec((b_ada.shape[0], ADA_PREP_COLS), lambda s: (0, ada_block(s))),
        ] + cast_in,
        out_specs=[
            pl.BlockSpec((None, tm, d), mlp_tile),
            pl.BlockSpec((POOL_BUF, n_seq, d), lambda s: (0, 0, 0)),
            pl.BlockSpec((c_all.shape[0], ADA_PREP_COLS), lambda s: (0, ada_block(s))),
        ] + cast_out,
        out_shape=[
            jax.ShapeDtypeStruct((n_seq, seq, d), F32),
            jax.ShapeDtypeStruct((POOL_BUF, n_seq, d), F32),
            jax.ShapeDtypeStruct((c_all.shape[0], n_mod), F32),
        ] + cast_shapes,
        scratch_shapes=[pltpu.VMEM((HIST + tm, d), F32)] + _pipeline_scratch(tm, d, d_ff),
        compiler_params=pltpu.CompilerParams(
            dimension_semantics=("arbitrary",), vmem_limit_bytes=VMEM_LIMIT),
        name="layer0_prompt",
    )(x, mod, ng, wp, ps, w1, w2, c_all, w_ada, b_ada, *[w for w, _ in next_weights])


def _layer0_sample(x, buf, mod, layer, ng, wp, ps, w1, w2):
    n_seq, seq_rows, d = x.shape
    d_ff = w1.shape[2]
    tb = SAMPLE_SEQS
    tm = tb * seq_rows
    return pl.pallas_call(
        functools.partial(_layer0_sample_kernel, seq_rows),
        grid=(n_seq // tb,),
        in_specs=[
            pl.BlockSpec((tb, seq_rows, d), lambda i: (i, 0, 0)),
            pl.BlockSpec((POOL_BUF, tb, d), lambda i: (0, i, 0)),
            pl.BlockSpec((tb, N_MOD * d), lambda i: (i, 0)),
            _resident(ng.shape, layer), _resident(wp.shape, 0), _resident(ps.shape, 0),
            _resident(w1.shape, 0), _resident(w2.shape, 0),
        ],
        out_specs=[pl.BlockSpec((tb, seq_rows, d), lambda i: (i, 0, 0)),
                   pl.BlockSpec((POOL_BUF, tb, d), lambda i: (0, i, 0))],
        out_shape=[jax.ShapeDtypeStruct(x.shape, F32), jax.ShapeDtypeStruct(buf.shape, F32)],
        scratch_shapes=[pltpu.VMEM((tm, d_ff), BF16)],
        compiler_params=pltpu.CompilerParams(
            dimension_semantics=("arbitrary",), vmem_limit_bytes=VMEM_LIMIT),
        name="layer0_sample",
    )(x, buf, mod, ng, wp, ps, w1, w2)


def _layer1_prompt_kernel(tiles_per_seq, x_ref, mod_ref, ng_ref, win_ref, bin_ref, lng_ref, lnb_ref,
                          wsp_ref, bsp_ref, wout_ref, w1_ref, w2_ref, fg_ref, o_ref, v_ref,
                          hb_ref, z_ref, vnb_ref, gate_ref, xn_next, x1_next, xn_cur, x1_cur, a_ref):
    tm, d = x_ref.shape
    ds = wout_ref.shape[0]
    v_rows = v_ref.shape[0]
    gs = ds // N_SGU_GROUPS

    def mixer_pieces(tile):
        sh1, sc1, gt1, sh2, sc2, _ = _split_mod(mod_ref[pl.ds(tile // tiles_per_seq, 1), :], d)

        def norm():
            h = _modulate(x_ref[...], ng_ref[0:1, :], sh1, sc1)
            hb_ref[...] = h.astype(BF16)
            return h

        def sgu_in(c):
            cols = slice(c * FF_COLS, (c + 1) * FF_COLS)
            z = jnp.dot(hb_ref[...], win_ref[:, cols], preferred_element_type=F32) + bin_ref[:, cols]
            z = _gelu(z)
            z_ref[:, cols] = z
            return z

        def norm_v():
            vn = _layer_norm(z_ref[:, ds:], lng_ref[...], lnb_ref[...])
            v_ref[...] = vn[tm - v_rows:]
            vnb_ref[...] = vn.astype(BF16)
            return vn

        def spatial():
            r = lax.broadcasted_iota(jnp.int32, (CHUNK, CHUNK), 0)
            c = lax.broadcasted_iota(jnp.int32, (CHUNK, CHUNK), 1)
            for g in range(N_SGU_GROUPS):
                wt = jnp.where(c <= r, wsp_ref[g], 0.0).astype(BF16)
                cols = slice(g * gs, (g + 1) * gs)
                for ci in range(tm // CHUNK):
                    rows = slice(ci * CHUNK, (ci + 1) * CHUNK)
                    mixed = jnp.dot(wt, vnb_ref[rows, cols], preferred_element_type=F32) + bsp_ref[:, cols]
                    gate_ref[rows, cols] = (z_ref[rows, cols] * mixed).astype(BF16)
            return mixed

        def sgu_out():
            x1 = x_ref[...] + gt1 * jnp.dot(gate_ref[...], wout_ref[...], preferred_element_type=F32)
            x1_next[...] = x1
            xn = _modulate(x1, ng_ref[1:2, :], sh2, sc2)
            xn_next[...] = xn.astype(BF16)
            return xn

        return ([norm] + [functools.partial(sgu_in, c) for c in range(2 * ds // FF_COLS)]
                + [norm_v, spatial, sgu_out])

    def mlp(tile, side_work):
        gt2 = mod_ref[pl.ds(tile // tiles_per_seq, 1), (N_MOD - 1) * d:]
        _handoff(xn_next, x1_next, xn_cur, x1_cur)
        x2 = x1_cur[...] + gt2 * _channel_mlp_with_side_work(xn_cur, w1_ref, w2_ref, a_ref, side_work)
        o_ref[...] = _rms_norm(x2, fg_ref[...])

    _pipelined_steps(lambda: None, mixer_pieces, mlp)


def _layer1_sample_kernel(seq_rows, wsp_ref, x_ref, mod_ref, ng_ref, win_ref, bin_ref, lng_ref, lnb_ref,
                          bsp_ref, wout_ref, w1_ref, w2_ref, fg_ref, o_ref, v_ref, z_ref, a_ref):
    n_seq, _, d = x_ref.shape
    ds = wout_ref.shape[0]
    sh1, sc1, gt1, sh2, sc2, gt2 = _split_mod(mod_ref[...], d)

    xt = jnp.swapaxes(x_ref[...], 0, 1)
    xs = [xt[t] for t in range(seq_rows)]
    h = jnp.concatenate([_modulate(x, ng_ref[0:1, :], sh1, sc1).astype(BF16) for x in xs], axis=0)
    _sgu_in(h, win_ref, bin_ref, z_ref)
    vn = _layer_norm(z_ref[:, ds:], lng_ref[...], lnb_ref[...])
    vs = [vn[t * n_seq:(t + 1) * n_seq] for t in range(seq_rows)]
    v_ref[...] = jnp.swapaxes(vn.reshape(seq_rows, n_seq, ds), 0, 1)

    gs = ds // N_SGU_GROUPS
    gate = []
    for t in range(seq_rows):
        parts = []
        for g in range(N_SGU_GROUPS):
            cols = slice(g * gs, (g + 1) * gs)
            mixed = bsp_ref[t:t + 1, cols]
            for s in range(t + 1):
                mixed = mixed + wsp_ref[(g * seq_rows + t) * seq_rows + s] * vs[s][:, cols]
            parts.append(mixed)
        gate.append((z_ref[t * n_seq:(t + 1) * n_seq, :ds] * jnp.concatenate(parts, axis=1)).astype(BF16))
    y = jnp.dot(jnp.concatenate(gate, axis=0), wout_ref[...], preferred_element_type=F32)
    x1 = [xs[t] + gt1 * y[t * n_seq:(t + 1) * n_seq] for t in range(seq_rows)]

    xn = jnp.concatenate([_modulate(x, ng_ref[1:2, :], sh2, sc2).astype(BF16) for x in x1], axis=0)
    y = _channel_mlp(xn, w1_ref, w2_ref, a_ref)
    out = [_rms_norm(x1[t] + gt2 * y[t * n_seq:(t + 1) * n_seq], fg_ref[...]) for t in range(seq_rows)]
    o_ref[...] = jnp.swapaxes(jnp.stack(out, axis=0), 0, 1)


def _layer1_weight_specs(layer, ng, win, b_in, lng, lnb, bias, wout, w1, w2, fg):
    return [_resident(ng.shape, layer), _resident(win.shape, 0), _resident(b_in.shape, 0),
            _resident(lng.shape, 0), _resident(lnb.shape, 0), _resident(bias.shape, 0),
            _resident(wout.shape, 0), _resident(w1.shape, 0), _resident(w2.shape, 0),
            _resident(fg.shape, 0)]


def _layer1_prompt(x, mod, n_lead, layer, ng, win, b_in, lng, lnb, wsp, bias, wout, w1, w2, fg):
    n_seq, seq, d = x.shape
    ds = wout.shape[1]
    d_ff = w1.shape[2]
    tm = TILE_ROWS
    v_rows = seq - CHUNK * ((seq - 1) // CHUNK)
    specs = _layer1_weight_specs(layer, ng, win, b_in, lng, lnb, bias, wout, w1, w2, fg)
    mixer_tile, mlp_tile, n_steps = _pipelined_tile_maps(n_seq, seq // tm)
    return pl.pallas_call(
        functools.partial(_layer1_prompt_kernel, seq // tm),
        grid=(n_steps,),
        in_specs=[
            pl.BlockSpec((None, tm, d), mixer_tile),
            pl.BlockSpec((n_seq, N_MOD * d), lambda s: (n_lead // n_seq, 0)),
        ] + specs[:5] + [_resident(wsp.shape, 0)] + specs[5:],
        out_specs=[
            pl.BlockSpec((None, tm, d), mlp_tile),
            pl.BlockSpec((None, v_rows, ds), lambda s: (mixer_tile(s)[0], 0, 0)),
        ],
        out_shape=[
            jax.ShapeDtypeStruct((n_seq, seq, d), F32),
            jax.ShapeDtypeStruct((n_seq, v_rows, ds), F32),
        ],
        scratch_shapes=[pltpu.VMEM((tm, d), BF16), pltpu.VMEM((tm, 2 * ds), F32), pltpu.VMEM((tm, ds), BF16),
                        pltpu.VMEM((tm, ds), BF16)] + _pipeline_scratch(tm, d, d_ff),
        compiler_params=pltpu.CompilerParams(
            dimension_semantics=("arbitrary",), vmem_limit_bytes=VMEM_LIMIT),
        name="layer1_prompt",
    )(x, mod, ng, win, b_in, lng, lnb, wsp, bias, wout, w1, w2, fg)


def _layer1_sample(x, mod, layer, wsp_corner, ng, win, b_in, lng, lnb, bias, wout, w1, w2, fg):
    n_seq, seq_rows, d = x.shape
    ds = wout.shape[1]
    d_ff = w1.shape[2]
    tb = SAMPLE_SEQS
    tm = tb * seq_rows
    return pl.pallas_call(
        functools.partial(_layer1_sample_kernel, seq_rows),
        grid=(n_seq // tb,),
        in_specs=[
            pl.BlockSpec(memory_space=pltpu.SMEM),
            pl.BlockSpec((tb, seq_rows, d), lambda i: (i, 0, 0)),
            pl.BlockSpec((tb, N_MOD * d), lambda i: (i, 0)),
        ] + _layer1_weight_specs(layer, ng, win, b_in, lng, lnb, bias, wout, w1, w2, fg),
        out_specs=[pl.BlockSpec((tb, seq_rows, d), lambda i: (i, 0, 0)),
                   pl.BlockSpec((tb, seq_rows, ds), lambda i: (i, 0, 0))],
        out_shape=[jax.ShapeDtypeStruct(x.shape, F32), jax.ShapeDtypeStruct((n_seq, seq_rows, ds), F32)],
        scratch_shapes=[pltpu.VMEM((tm, 2 * ds), F32), pltpu.VMEM((tm, d_ff), BF16)],
        compiler_params=pltpu.CompilerParams(
            dimension_semantics=("arbitrary",), vmem_limit_bytes=VMEM_LIMIT),
        name="layer1_sample",
    )(wsp_corner, x, mod, ng, win, b_in, lng, lnb, bias, wout, w1, w2, fg)


def kernel(x_prompt, x_sample, c_prompt, c_sample, state_pool, norm_g, w_ada, b_ada, w_pool, pool_scale,
           sgu_w_in, sgu_b_in, sgu_ln_g, sgu_ln_b, sgu_w_sp, sgu_b_sp, sgu_w_out, mlp_w1, mlp_w2, final_g):
    n_p, seq, d = x_prompt.shape
    n_s, dec_seq, _ = x_sample.shape
    assert norm_g.shape[0] == 2 and state_pool.shape[0] == 1 and sgu_w_in.shape[0] == 1
    assert state_pool.shape[2] == POOL_BUF and dec_seq <= POOL_BUF and dec_seq <= CHUNK
    assert seq % TILE_ROWS == 0 and TILE_ROWS % CHUNK == 0 and TILE_ROWS >= HIST
    assert n_s % SAMPLE_SEQS == 0 and n_s % n_p == 0

    c_all = jnp.concatenate([c_sample, c_prompt], axis=0)
    mod0 = _ada_modulation(c_all, w_ada, b_ada, 0)

    l0 = (norm_g, w_pool.astype(BF16), pool_scale, mlp_w1[0:1].astype(BF16), mlp_w2[0:1].astype(BF16))
    xp, pool_p, mod1, w_in, w_out, w1, w2 = _layer0_prompt(
        x_prompt, mod0, n_s, 0, *l0, c_all, w_ada, b_ada,
        [(sgu_w_in, 0), (sgu_w_out, 0), (mlp_w1, 1), (mlp_w2, 1)])
    buf = jnp.transpose(state_pool[0], (1, 0, 2))
    xs, pool_s = _layer0_sample(x_sample, buf, mod0, 0, *l0)

    ds = sgu_w_out.shape[1]
    bias = jnp.repeat(sgu_b_sp[0].T, ds // N_SGU_GROUPS, axis=1)[None]
    l1 = (norm_g, w_in, sgu_b_in, sgu_ln_g, sgu_ln_b, bias, w_out, w1, w2, final_g[None])
    yp, v_p = _layer1_prompt(xp, mod1, n_s, 1, *l1[:5], sgu_w_sp, *l1[5:])
    corner = sgu_w_sp[0][:, :dec_seq, :dec_seq].reshape(-1)
    ys, v_s = _layer1_sample(xs, mod1, 1, corner, *l1)

    return (yp, ys,
            jnp.transpose(pool_p, (1, 0, 2))[None], jnp.transpose(pool_s, (1, 0, 2))[None],
            v_p[None], v_s[None])
```

```python
import functools

import jax
import jax.numpy as jnp
from jax import lax
from jax.experimental import pallas as pl
from jax.experimental.pallas import tpu as pltpu

F32 = jnp.float32
BF16 = jnp.bfloat16

POOL_WINDOWS = (2, 4, 8, 16)
POOL_BUF = max(POOL_WINDOWS) - 1
HIST = 16
CHUNK = 128
N_SGU_GROUPS = 4
N_MOD = 6
EPS = 1e-6

TILE_ROWS = 512
SAMPLE_SEQS = 64
FF_COLS = 512
ADA_COLS = 768
ADA_PREP_COLS = 256
VMEM_LIMIT = 62 * 1024 * 1024


def _rms_norm(x, g):
    ms = jnp.mean(x * x, axis=-1, keepdims=True)
    return x * lax.rsqrt(ms + EPS) * g


def _modulate(x, g, shift, scale):
    return _rms_norm(x, g) * (1 + scale) + shift


def _gelu(z):
    return 0.5 * z * (1 + lax.erf(z * (0.5 ** 0.5)))


def _layer_norm(v, g, b):
    mu = jnp.mean(v, axis=-1, keepdims=True)
    vc = v - mu
    var = jnp.mean(vc * vc, axis=-1, keepdims=True)
    return vc * lax.rsqrt(var + EPS) * g + b


def _resident(shape, layer):
    if len(shape) == 2:
        assert shape[0] == 1 and layer == 0
        return pl.BlockSpec(tuple(shape), lambda *_: (0, 0), pipeline_mode=pl.Buffered(1))
    return pl.BlockSpec((None,) + tuple(shape[1:]), lambda *_: (layer,) + (0,) * (len(shape) - 1),
                        pipeline_mode=pl.Buffered(1))


def _split_mod(mod, d):
    return [mod[:, k * d:(k + 1) * d] for k in range(N_MOD)]


def _channel_mlp(xn, w1_ref, w2_ref, a_ref):
    d_ff = w1_ref.shape[1]
    for c in range(d_ff // FF_COLS):
        cols = slice(c * FF_COLS, (c + 1) * FF_COLS)
        a = jnp.maximum(jnp.dot(xn, w1_ref[:, cols], preferred_element_type=F32), 0.0)
        a_ref[:, cols] = (a * a).astype(BF16)
    return jnp.dot(a_ref[...], w2_ref[...], preferred_element_type=F32)


def _pace(lhs_ref, v):
    bits = pltpu.bitcast(v[0:8, 0:128].astype(F32), jnp.int32)
    zero = lax.shift_right_logical(lax.shift_right_logical(bits, 16), 16).astype(F32)
    rows = 32 // lhs_ref.dtype.itemsize
    zero = jnp.concatenate([zero] * (rows // 8), axis=0).astype(lhs_ref.dtype)
    lhs_ref[0:rows, 0:128] = lhs_ref[0:rows, 0:128] + zero


def _channel_mlp_with_side_work(xn_ref, w1_ref, w2_ref, a_ref, side_work):
    side = list(side_work)
    d_ff = w1_ref.shape[1]
    n_chunks = d_ff // FF_COLS
    assert len(side) <= n_chunks
    for c in range(n_chunks):
        cols = slice(c * FF_COLS, (c + 1) * FF_COLS)
        a = jnp.maximum(jnp.dot(xn_ref[...], w1_ref[:, cols], preferred_element_type=F32), 0.0)
        a_ref[:, cols] = (a * a).astype(BF16)
        if side:
            _pace(xn_ref if c + 1 < n_chunks else a_ref, side.pop(0)())
    return jnp.dot(a_ref[...], w2_ref[...], preferred_element_type=F32)


def _pool_project(delta, wp_ref, ps_ref):
    ys = [jnp.dot(dl, wp_ref[g].astype(BF16), preferred_element_type=F32) for g, dl in enumerate(delta)]
    return jnp.concatenate(ys, axis=1) * ps_ref[...]


def _sgu_in(h, win_ref, bin_ref, z_ref):
    for c in range(win_ref.shape[1] // FF_COLS):
        cols = slice(c * FF_COLS, (c + 1) * FF_COLS)
        z = jnp.dot(h, win_ref[:, cols], preferred_element_type=F32) + bin_ref[:, cols]
        z_ref[:, cols] = _gelu(z)


def _ada_rows(c_ref, w_ref, b_row):
    s = jax.nn.silu(c_ref[...]).astype(BF16)
    return jnp.dot(s, w_ref[...].astype(BF16), preferred_element_type=F32) + b_row


def _first_layer_prep_kernel(layer, c_ref, w_ref, b_ref, *refs):
    n_cast = (len(refs) - 1) // 2
    o_ref = refs[n_cast]
    o_ref[...] = _ada_rows(c_ref, w_ref, b_ref[layer:layer + 1, :])
    for src, dst in zip(refs[:n_cast], refs[n_cast + 1:]):
        dst[...] = src[...].astype(BF16)


def _first_layer_prep(c_all, w_ada, b_ada, layer, weights):
    depth, d, n = w_ada.shape
    rows = c_all.shape[0]
    n_steps = n // ADA_COLS
    cast_in, cast_out, cast_shapes = [], [], []
    for w in weights:
        r = w.shape[1] // n_steps
        assert w.shape[1] % n_steps == 0 and r % 16 == 0
        cast_in.append(pl.BlockSpec((None, r, w.shape[2]), lambda j: (layer, j, 0)))
        cast_out.append(pl.BlockSpec((None, r, w.shape[2]), lambda j: (0, j, 0)))
        cast_shapes.append(jax.ShapeDtypeStruct((1,) + w.shape[1:], BF16))
    return pl.pallas_call(
        functools.partial(_first_layer_prep_kernel, layer),
        grid=(n_steps,),
        in_specs=[
            pl.BlockSpec((rows, d), lambda j: (0, 0)),
            pl.BlockSpec((None, d, ADA_COLS), lambda j: (layer, 0, j)),
            pl.BlockSpec((depth, ADA_COLS), lambda j: (0, j)),
        ] + cast_in,
        out_specs=[pl.BlockSpec((rows, ADA_COLS), lambda j: (0, j))] + cast_out,
        out_shape=[jax.ShapeDtypeStruct((rows, n), F32)] + cast_shapes,
        compiler_params=pltpu.CompilerParams(
            dimension_semantics=("arbitrary",), vmem_limit_bytes=VMEM_LIMIT),
        name="first_layer_prep",
    )(c_all, w_ada, b_ada, *weights)


TILES_PER_STEP = 2


def _pipelined_steps(init, mixer_pieces, mlp, x_steps):
    k = pl.program_id(0)
    n_tiles = TILES_PER_STEP * (pl.num_programs(0) - 1)

    @pl.when(k == 0)
    def _():
        init()
        for piece in mixer_pieces(k, x_steps[-1], 0):
            piece()

    @pl.when(k > 0)
    def _():
        for i in range(TILES_PER_STEP):
            t = (k - 1) * TILES_PER_STEP + i
            mlp(t, i % 2, i, mixer_pieces(jnp.minimum(t + 1, n_tiles - 1), x_steps[i], (i + 1) % 2))


def _pipelined_tile_maps(n_seq, tiles_per_seq):
    n_tiles = n_seq * tiles_per_seq
    assert TILES_PER_STEP == 2 and tiles_per_seq % TILES_PER_STEP == 0

    def mixer_tile(i):
        def index(k):
            t = jnp.clip(k * TILES_PER_STEP - 1 + i, 0, n_tiles - 1)
            return t // tiles_per_seq, t % tiles_per_seq, 0
        return index

    def out_block(k):
        b = jnp.maximum(k - 1, 0)
        steps_per_seq = tiles_per_seq // TILES_PER_STEP
        return b // steps_per_seq, b % steps_per_seq, 0

    return ([mixer_tile(i) for i in range(TILES_PER_STEP)], out_block,
            lambda k: mixer_tile(0)(k)[0], n_tiles // TILES_PER_STEP + 1)


def _pipeline_scratch(tm, d, d_ff):
    return [pltpu.VMEM((tm, d), BF16), pltpu.VMEM((tm, d), BF16), pltpu.VMEM((tm, d), F32),
            pltpu.VMEM((tm, d), F32), pltpu.VMEM((tm, d_ff), BF16)]


def _layer0_prompt_kernel(tiles_per_seq, next_layer, xa_ref, xb_ref, mod_ref, ng_ref, wp_ref, ps_ref,
                          w1_ref, w2_ref, c_ref, wada_ref, bada_ref, *rest):
    n_cast = (len(rest) - 9) // 2
    cast_src = rest[:n_cast]
    o_ref, tail_ref, mod_next_ref = rest[n_cast:n_cast + 3]
    cast_dst = rest[n_cast + 3:2 * n_cast + 3]
    hh_ref, xn0, xn1, x10, x11, a_ref = rest[2 * n_cast + 3:]
    xn_sets, x1_sets = (xn0, xn1), (x10, x11)
    tm, d = xa_ref.shape
    pg = d // len(POOL_WINDOWS)

    def next_layer_prep():
        mod_next = _ada_rows(c_ref, wada_ref, bada_ref[next_layer:next_layer + 1, :])
        mod_next_ref[...] = mod_next
        for src, dst in zip(cast_src, cast_dst):
            dst[...] = src[...].astype(BF16)
        return mod_next

    def mixer_pieces(tile, x_ref, buffer_set):
        b, j = tile // tiles_per_seq, tile % tiles_per_seq
        sh1, sc1, gt1, sh2, sc2, _ = _split_mod(mod_ref[pl.ds(b, 1), :], d)
        first = j == 0
        xn_next, x1_next = xn_sets[buffer_set], x1_sets[buffer_set]

        def norm():
            h = _modulate(x_ref[...], ng_ref[0:1, :], sh1, sc1)
            hh_ref[0:HIST] = jnp.where(first, 0.0, hh_ref[tm:tm + HIST])
            hh_ref[HIST:] = h
            for k in range(POOL_BUF):
                tail_ref[k, pl.ds(b, 1), :] = h[tm - POOL_BUF + k:tm - POOL_BUF + k + 1]
            return h

        def group(g):
            w = POOL_WINDOWS[g]
            cols = slice(g * pg, (g + 1) * pg)
            s = hh_ref[:, cols]
            span = 1
            while span < w:
                s = s + pltpu.roll(s, span, axis=0)
                span *= 2
            s = s[HIST:]
            r = lax.broadcasted_iota(jnp.int32, (HIST, pg), 0)
            cnt = jnp.where(first, jnp.minimum(w, r + 1), w).astype(F32)
            m = jnp.concatenate([s[:HIST] / cnt, s[HIST:] * (1.0 / w)], axis=0)
            delta = (m - hh_ref[HIST:, cols]).astype(BF16)
            y = jnp.dot(delta, wp_ref[g].astype(BF16), preferred_element_type=F32) * ps_ref[:, cols]
            x1 = x_ref[:, cols] + gt1[:, cols] * y
            x1_next[:, cols] = x1
            return x1

        def norm2():
            xn = _modulate(x1_next[...], ng_ref[1:2, :], sh2, sc2)
            xn_next[...] = xn.astype(BF16)
            return xn

        pieces = [norm] + [functools.partial(group, g) for g in range(len(POOL_WINDOWS))] + [norm2]
        return pieces + [next_layer_prep] if x_ref is xa_ref else pieces

    def mlp(tile, buffer_set, slot, side_work):
        gt2 = mod_ref[pl.ds(tile // tiles_per_seq, 1), (N_MOD - 1) * d:]
        y = _channel_mlp_with_side_work(xn_sets[buffer_set], w1_ref, w2_ref, a_ref, side_work)
        o_ref[slot * tm:(slot + 1) * tm, :] = x1_sets[buffer_set][...] + gt2 * y

    def init():
        hh_ref[...] = jnp.zeros_like(hh_ref)

    _pipelined_steps(init, mixer_pieces, mlp, (xa_ref, xb_ref))


def _layer0_sample_kernel(seq_rows, x_ref, buf_ref, mod_ref, ng_ref, wp_ref, ps_ref, w1_ref, w2_ref,
                          o_ref, pool_ref, a_ref):
    n_seq, _, d = x_ref.shape
    sh1, sc1, gt1, sh2, sc2, gt2 = _split_mod(mod_ref[...], d)

    xt = jnp.swapaxes(x_ref[...], 0, 1)
    xs = [xt[t] for t in range(seq_rows)]
    hs = [_modulate(x, ng_ref[0:1, :], sh1, sc1) for x in xs]
    rows = [buf_ref[k] for k in range(POOL_BUF)] + hs
    for k in range(POOL_BUF):
        pool_ref[k] = rows[len(rows) - POOL_BUF + k]

    pg = d // len(POOL_WINDOWS)
    delta = []
    for g, w in enumerate(POOL_WINDOWS):
        cols = slice(g * pg, (g + 1) * pg)
        sums = {i: rows[i][:, cols] for i in range(len(rows))}
        span = 1
        while span < w:
            first = POOL_BUF - (w - 2 * span)
            sums = {i: sums[i] + sums[i - span] for i in range(first, len(rows))}
            span *= 2
        delta.append(jnp.concatenate(
            [(sums[POOL_BUF + t] * (1.0 / w) - hs[t][:, cols]).astype(BF16) for t in range(seq_rows)], axis=0))
    y = _pool_project(delta, wp_ref, ps_ref)
    x1 = [xs[t] + gt1 * y[t * n_seq:(t + 1) * n_seq] for t in range(seq_rows)]

    xn = jnp.concatenate([_modulate(x, ng_ref[1:2, :], sh2, sc2).astype(BF16) for x in x1], axis=0)
    y = _channel_mlp(xn, w1_ref, w2_ref, a_ref)
    out = [x1[t] + gt2 * y[t * n_seq:(t + 1) * n_seq] for t in range(seq_rows)]
    o_ref[...] = jnp.swapaxes(jnp.stack(out, axis=0), 0, 1)


def _layer0_prompt(x, mod, n_lead, layer, ng, wp, ps, w1, w2, c_all, w_ada, b_ada, next_weights):
    n_seq, seq, d = x.shape
    d_ff = w1.shape[2]
    tm = TILE_ROWS
    mixer_tiles, out_block, _, n_steps = _pipelined_tile_maps(n_seq, seq // tm)
    n_prep = n_steps - 1
    prep_block = lambda k: jnp.maximum(k - 1, 0)
    n_mod = w_ada.shape[2]
    ada_cols = n_mod // n_prep
    assert n_mod % n_prep == 0 and ada_cols % 128 == 0
    cast_in, cast_out, cast_shapes = [], [], []
    for w, l in next_weights:
        rows = w.shape[1] // n_prep
        assert w.shape[1] % n_prep == 0 and rows % 16 == 0
        cast_in.append(pl.BlockSpec((None, rows, w.shape[2]), lambda k, l=l: (l, prep_block(k), 0)))
        cast_out.append(pl.BlockSpec((None, rows, w.shape[2]), lambda k: (0, prep_block(k), 0)))
        cast_shapes.append(jax.ShapeDtypeStruct((1,) + w.shape[1:], BF16))
    return pl.pallas_call(
        functools.partial(_layer0_prompt_kernel, seq // tm, layer + 1),
        grid=(n_steps,),
        in_specs=[pl.BlockSpec((None, tm, d), m) for m in mixer_tiles] + [
            pl.BlockSpec((n_seq, N_MOD * d), lambda k: (n_lead // n_seq, 0)),
            _resident(ng.shape, layer), _resident(wp.shape, 0), _resident(ps.shape, 0),
            _resident(w1.shape, 0), _resident(w2.shape, 0),
            pl.BlockSpec(c_all.shape, lambda k: (0, 0)),
            pl.BlockSpec((None, d, ada_cols), lambda k: (layer + 1, 0, prep_block(k))),
            pl.BlockSpec((b_ada.shape[0], ada_cols), lambda k: (0, prep_block(k))),
        ] + cast_in,
        out_specs=[
            pl.BlockSpec((None, TILES_PER_STEP * tm, d), out_block),
            pl.BlockSpec((POOL_BUF, n_seq, d), lambda k: (0, 0, 0)),
            pl.BlockSpec((c_all.shape[0], ada_cols), lambda k: (0, prep_block(k))),
        ] + cast_out,
        out_shape=[
            jax.ShapeDtypeStruct((n_seq, seq, d), F32),
            jax.ShapeDtypeStruct((POOL_BUF, n_seq, d), F32),
            jax.ShapeDtypeStruct((c_all.shape[0], n_mod), F32),
        ] + cast_shapes,
        scratch_shapes=[pltpu.VMEM((HIST + tm, d), F32)] + _pipeline_scratch(tm, d, d_ff),
        compiler_params=pltpu.CompilerParams(
            dimension_semantics=("arbitrary",), vmem_limit_bytes=VMEM_LIMIT),
        name="layer0_prompt",
    )(x, x, mod, ng, wp, ps, w1, w2, c_all, w_ada, b_ada, *[w for w, _ in next_weights])


def _layer0_sample(x, buf, mod, layer, ng, wp, ps, w1, w2):
    n_seq, seq_rows, d = x.shape
    d_ff = w1.shape[2]
    tb = SAMPLE_SEQS
    tm = tb * seq_rows
    return pl.pallas_call(
        functools.partial(_layer0_sample_kernel, seq_rows),
        grid=(n_seq // tb,),
        in_specs=[
            pl.BlockSpec((tb, seq_rows, d), lambda i: (i, 0, 0)),
            pl.BlockSpec((POOL_BUF, tb, d), lambda i: (0, i, 0)),
            pl.BlockSpec((tb, N_MOD * d), lambda i: (i, 0)),
            _resident(ng.shape, layer), _resident(wp.shape, 0), _resident(ps.shape, 0),
            _resident(w1.shape, 0), _resident(w2.shape, 0),
        ],
        out_specs=[pl.BlockSpec((tb, seq_rows, d), lambda i: (i, 0, 0)),
                   pl.BlockSpec((POOL_BUF, tb, d), lambda i: (0, i, 0))],
        out_shape=[jax.ShapeDtypeStruct(x.shape, F32), jax.ShapeDtypeStruct(buf.shape, F32)],
        scratch_shapes=[pltpu.VMEM((tm, d_ff), BF16)],
        compiler_params=pltpu.CompilerParams(
            dimension_semantics=("arbitrary",), vmem_limit_bytes=VMEM_LIMIT),
        name="layer0_sample",
    )(x, buf, mod, ng, wp, ps, w1, w2)


def _layer1_prompt_kernel(tiles_per_seq, xa_ref, xb_ref, mod_ref, ng_ref, win_ref, bin_ref, lng_ref, lnb_ref,
                          wsp_ref, bsp_ref, wout_ref, w1_ref, w2_ref, fg_ref, o_ref, v_ref,
                          hb_ref, z_ref, vnb_ref, gate_ref, xn0, xn1, x10, x11, a_ref):
    tm, d = xa_ref.shape
    ds = wout_ref.shape[0]
    v_rows = v_ref.shape[0]
    gs = ds // N_SGU_GROUPS
    xn_sets, x1_sets = (xn0, xn1), (x10, x11)

    def mixer_pieces(tile, x_ref, buffer_set):
        sh1, sc1, gt1, sh2, sc2, _ = _split_mod(mod_ref[pl.ds(tile // tiles_per_seq, 1), :], d)
        xn_next, x1_next = xn_sets[buffer_set], x1_sets[buffer_set]

        def norm():
            h = _modulate(x_ref[...], ng_ref[0:1, :], sh1, sc1)
            hb_ref[...] = h.astype(BF16)
            return h

        def sgu_in(c):
            cols = slice(c * FF_COLS, (c + 1) * FF_COLS)
            z = jnp.dot(hb_ref[...], win_ref[:, cols], preferred_element_type=F32) + bin_ref[:, cols]
            z = _gelu(z)
            z_ref[:, cols] = z
            return z

        def norm_v():
            vn = _layer_norm(z_ref[:, ds:], lng_ref[...], lnb_ref[...])
            if x_ref is xa_ref:
                v_ref[...] = vn[tm - v_rows:]
            vnb_ref[...] = vn.astype(BF16)
            return vn

        def spatial():
            r = lax.broadcasted_iota(jnp.int32, (CHUNK, CHUNK), 0)
            c = lax.broadcasted_iota(jnp.int32, (CHUNK, CHUNK), 1)
            for g in range(N_SGU_GROUPS):
                wt = jnp.where(c <= r, wsp_ref[g], 0.0).astype(BF16)
                cols = slice(g * gs, (g + 1) * gs)
                for ci in range(tm // CHUNK):
                    rows = slice(ci * CHUNK, (ci + 1) * CHUNK)
                    mixed = jnp.dot(wt, vnb_ref[rows, cols], preferred_element_type=F32) + bsp_ref[:, cols]
                    gate_ref[rows, cols] = (z_ref[rows, cols] * mixed).astype(BF16)
            return mixed

        def sgu_out():
            x1 = x_ref[...] + gt1 * jnp.dot(gate_ref[...], wout_ref[...], preferred_element_type=F32)
            x1_next[...] = x1
            xn = _modulate(x1, ng_ref[1:2, :], sh2, sc2)
            xn_next[...] = xn.astype(BF16)
            return xn

        return ([norm] + [functools.partial(sgu_in, c) for c in range(2 * ds // FF_COLS)]
                + [norm_v, spatial, sgu_out])

    def mlp(tile, buffer_set, slot, side_work):
        gt2 = mod_ref[pl.ds(tile // tiles_per_seq, 1), (N_MOD - 1) * d:]
        y = _channel_mlp_with_side_work(xn_sets[buffer_set], w1_ref, w2_ref, a_ref, side_work)
        o_ref[slot * tm:(slot + 1) * tm, :] = _rms_norm(x1_sets[buffer_set][...] + gt2 * y, fg_ref[...])

    _pipelined_steps(lambda: None, mixer_pieces, mlp, (xa_ref, xb_ref))


def _layer1_sample_kernel(seq_rows, wsp_ref, x_ref, mod_ref, ng_ref, win_ref, bin_ref, lng_ref, lnb_ref,
                          bsp_ref, wout_ref, w1_ref, w2_ref, fg_ref, o_ref, v_ref, z_ref, a_ref):
    n_seq, _, d = x_ref.shape
    ds = wout_ref.shape[0]
    sh1, sc1, gt1, sh2, sc2, gt2 = _split_mod(mod_ref[...], d)

    xt = jnp.swapaxes(x_ref[...], 0, 1)
    xs = [xt[t] for t in range(seq_rows)]
    h = jnp.concatenate([_modulate(x, ng_ref[0:1, :], sh1, sc1).astype(BF16) for x in xs], axis=0)
    _sgu_in(h, win_ref, bin_ref, z_ref)
    vn = _layer_norm(z_ref[:, ds:], lng_ref[...], lnb_ref[...])
    vs = [vn[t * n_seq:(t + 1) * n_seq] for t in range(seq_rows)]
    v_ref[...] = jnp.swapaxes(vn.reshape(seq_rows, n_seq, ds), 0, 1)

    gs = ds // N_SGU_GROUPS
    gate = []
    for t in range(seq_rows):
        parts = []
        for g in range(N_SGU_GROUPS):
            cols = slice(g * gs, (g + 1) * gs)
            mixed = bsp_ref[t:t + 1, cols]
            for s in range(t + 1):
                mixed = mixed + wsp_ref[(g * seq_rows + t) * seq_rows + s] * vs[s][:, cols]
            parts.append(mixed)
        gate.append((z_ref[t * n_seq:(t + 1) * n_seq, :ds] * jnp.concatenate(parts, axis=1)).astype(BF16))
    y = jnp.dot(jnp.concatenate(gate, axis=0), wout_ref[...], preferred_element_type=F32)
    x1 = [xs[t] + gt1 * y[t * n_seq:(t + 1) * n_seq] for t in range(seq_rows)]

    xn = jnp.concatenate([_modulate(x, ng_ref[1:2, :], sh2, sc2).astype(BF16) for x in x1], axis=0)
    y = _channel_mlp(xn, w1_ref, w2_ref, a_ref)
    out = [_rms_norm(x1[t] + gt2 * y[t * n_seq:(t + 1) * n_seq], fg_ref[...]) for t in range(seq_rows)]
    o_ref[...] = jnp.swapaxes(jnp.stack(out, axis=0), 0, 1)


def _layer1_weight_specs(layer, ng, win, b_in, lng, lnb, bias, wout, w1, w2, fg):
    return [_resident(ng.shape, layer), _resident(win.shape, 0), _resident(b_in.shape, 0),
            _resident(lng.shape, 0), _resident(lnb.shape, 0), _resident(bias.shape, 0),
            _resident(wout.shape, 0), _resident(w1.shape, 0), _resident(w2.shape, 0),
            _resident(fg.shape, 0)]


def _layer1_prompt(x, mod, n_lead, layer, ng, win, b_in, lng, lnb, wsp, bias, wout, w1, w2, fg):
    n_seq, seq, d = x.shape
    ds = wout.shape[1]
    d_ff = w1.shape[2]
    tm = TILE_ROWS
    v_rows = seq - CHUNK * ((seq - 1) // CHUNK)
    specs = _layer1_weight_specs(layer, ng, win, b_in, lng, lnb, bias, wout, w1, w2, fg)
    mixer_tiles, out_block, first_mixer_seq, n_steps = _pipelined_tile_maps(n_seq, seq // tm)
    assert (seq // tm) % TILES_PER_STEP == 0
    return pl.pallas_call(
        functools.partial(_layer1_prompt_kernel, seq // tm),
        grid=(n_steps,),
        in_specs=[pl.BlockSpec((None, tm, d), m) for m in mixer_tiles] + [
            pl.BlockSpec((n_seq, N_MOD * d), lambda k: (n_lead // n_seq, 0)),
        ] + specs[:5] + [_resident(wsp.shape, 0)] + specs[5:],
        out_specs=[
            pl.BlockSpec((None, TILES_PER_STEP * tm, d), out_block),
            pl.BlockSpec((None, v_rows, ds), lambda k: (first_mixer_seq(k), 0, 0)),
        ],
        out_shape=[
            jax.ShapeDtypeStruct((n_seq, seq, d), F32),
            jax.ShapeDtypeStruct((n_seq, v_rows, ds), F32),
        ],
        scratch_shapes=[pltpu.VMEM((tm, d), BF16), pltpu.VMEM((tm, 2 * ds), F32), pltpu.VMEM((tm, ds), BF16),
                        pltpu.VMEM((tm, ds), BF16)] + _pipeline_scratch(tm, d, d_ff),
        compiler_params=pltpu.CompilerParams(
            dimension_semantics=("arbitrary",), vmem_limit_bytes=VMEM_LIMIT),
        name="layer1_prompt",
    )(x, x, mod, ng, win, b_in, lng, lnb, wsp, bias, wout, w1, w2, fg)


def _layer1_sample(x, mod, layer, wsp_corner, ng, win, b_in, lng, lnb, bias, wout, w1, w2, fg):
    n_seq, seq_rows, d = x.shape
    ds = wout.shape[1]
    d_ff = w1.shape[2]
    tb = SAMPLE_SEQS
    tm = tb * seq_rows
    return pl.pallas_call(
        functools.partial(_layer1_sample_kernel, seq_rows),
        grid=(n_seq // tb,),
        in_specs=[
            pl.BlockSpec(memory_space=pltpu.SMEM),
            pl.BlockSpec((tb, seq_rows, d), lambda i: (i, 0, 0)),
            pl.BlockSpec((tb, N_MOD * d), lambda i: (i, 0)),
        ] + _layer1_weight_specs(layer, ng, win, b_in, lng, lnb, bias, wout, w1, w2, fg),
        out_specs=[pl.BlockSpec((tb, seq_rows, d), lambda i: (i, 0, 0)),
                   pl.BlockSpec((tb, seq_rows, ds), lambda i: (i, 0, 0))],
        out_shape=[jax.ShapeDtypeStruct(x.shape, F32), jax.ShapeDtypeStruct((n_seq, seq_rows, ds), F32)],
        scratch_shapes=[pltpu.VMEM((tm, 2 * ds), F32), pltpu.VMEM((tm, d_ff), BF16)],
        compiler_params=pltpu.CompilerParams(
            dimension_semantics=("arbitrary",), vmem_limit_bytes=VMEM_LIMIT),
        name="layer1_sample",
    )(wsp_corner, x, mod, ng, win, b_in, lng, lnb, bias, wout, w1, w2, fg)


def kernel(x_prompt, x_sample, c_prompt, c_sample, state_pool, norm_g, w_ada, b_ada, w_pool, pool_scale,
           sgu_w_in, sgu_b_in, sgu_ln_g, sgu_ln_b, sgu_w_sp, sgu_b_sp, sgu_w_out, mlp_w1, mlp_w2, final_g):
    n_p, seq, d = x_prompt.shape
    n_s, dec_seq, _ = x_sample.shape
    assert norm_g.shape[0] == 2 and state_pool.shape[0] == 1 and sgu_w_in.shape[0] == 1
    assert state_pool.shape[2] == POOL_BUF and dec_seq <= POOL_BUF and dec_seq <= CHUNK
    assert seq % TILE_ROWS == 0 and TILE_ROWS % CHUNK == 0 and TILE_ROWS >= HIST
    assert n_s % SAMPLE_SEQS == 0 and n_s % n_p == 0

    c_all = jnp.concatenate([c_sample, c_prompt], axis=0)
    mod0, w1_0, w2_0 = _first_layer_prep(c_all, w_ada, b_ada, 0, [mlp_w1, mlp_w2])

    l0 = (norm_g, w_pool, pool_scale, w1_0, w2_0)
    xp, pool_p, mod1, w_in, w_out, w1, w2 = _layer0_prompt(
        x_prompt, mod0, n_s, 0, *l0, c_all, w_ada, b_ada,
        [(sgu_w_in, 0), (sgu_w_out, 0), (mlp_w1, 1), (mlp_w2, 1)])
    buf = jnp.transpose(state_pool[0], (1, 0, 2))
    xs, pool_s = _layer0_sample(x_sample, buf, mod0, 0, *l0)

    ds = sgu_w_out.shape[1]
    bias = jnp.repeat(sgu_b_sp[0].T, ds // N_SGU_GROUPS, axis=1)[None]
    l1 = (norm_g, w_in, sgu_b_in, sgu_ln_g, sgu_ln_b, bias, w_out, w1, w2, final_g[None])
    yp, v_p = _layer1_prompt(xp, mod1, n_s, 1, *l1[:5], sgu_w_sp, *l1[5:])
    corner = sgu_w_sp[0][:, :dec_seq, :dec_seq].reshape(-1)
    ys, v_s = _layer1_sample(xs, mod1, 1, corner, *l1)

    return (yp, ys,
            jnp.transpose(pool_p, (1, 0, 2))[None], jnp.transpose(pool_s, (1, 0, 2))[None],
            v_p[None], v_s[None])
```

```python
import functools

import jax
import jax.numpy as jnp
from jax import lax
from jax.experimental import pallas as pl
from jax.experimental.pallas import tpu as pltpu

F32 = jnp.float32
BF16 = jnp.bfloat16

POOL_WINDOWS = (2, 4, 8, 16)
POOL_BUF = max(POOL_WINDOWS) - 1
HIST = 16
CHUNK = 128
N_SGU_GROUPS = 4
N_MOD = 6
EPS = 1e-6

TILE_ROWS = 512
SAMPLE_SEQS = 64
FF_COLS = 512
DOWN_COLS = 512
ADA_COLS = 768
ADA_PREP_COLS = 256
VMEM_LIMIT = 56 * 1024 * 1024


def _rms_norm(x, g):
    ms = jnp.mean(x * x, axis=-1, keepdims=True)
    return x * lax.rsqrt(ms + EPS) * g


def _modulate(x, g, shift, scale):
    return _rms_norm(x, g) * (1 + scale) + shift


def _gelu(z):
    return 0.5 * z * (1 + lax.erf(z * (0.5 ** 0.5)))


def _layer_norm(v, g, b):
    mu = jnp.mean(v, axis=-1, keepdims=True)
    vc = v - mu
    var = jnp.mean(vc * vc, axis=-1, keepdims=True)
    return vc * lax.rsqrt(var + EPS) * g + b


def _resident(shape, layer):
    if len(shape) == 2:
        assert shape[0] == 1 and layer == 0
        return pl.BlockSpec(tuple(shape), lambda *_: (0, 0), pipeline_mode=pl.Buffered(1))
    return pl.BlockSpec((None,) + tuple(shape[1:]), lambda *_: (layer,) + (0,) * (len(shape) - 1),
                        pipeline_mode=pl.Buffered(1))


def _split_mod(mod, d):
    return [mod[:, k * d:(k + 1) * d] for k in range(N_MOD)]


def _channel_mlp(xn, w1_ref, w2_ref, a_ref):
    d_ff = w1_ref.shape[1]
    for c in range(d_ff // FF_COLS):
        cols = slice(c * FF_COLS, (c + 1) * FF_COLS)
        a = jnp.maximum(jnp.dot(xn, w1_ref[:, cols], preferred_element_type=F32), 0.0)
        a_ref[:, cols] = (a * a).astype(BF16)
    return jnp.dot(a_ref[...], w2_ref[...], preferred_element_type=F32)


def _pace(lhs_ref, v):
    bits = pltpu.bitcast(v[0:8, 0:128].astype(F32), jnp.int32)
    zero = lax.shift_right_logical(lax.shift_right_logical(bits, 16), 16).astype(F32)
    rows = 32 // lhs_ref.dtype.itemsize
    zero = jnp.concatenate([zero] * (rows // 8), axis=0).astype(lhs_ref.dtype)
    lhs_ref[0:rows, 0:128] = lhs_ref[0:rows, 0:128] + zero


def _channel_mlp_with_side_work(xn_ref, w1_ref, w2_ref, a_ref, side_work, emit):
    side = list(side_work)
    d, d_ff = w1_ref.shape

    def up(cols):
        a = jnp.maximum(jnp.dot(xn_ref[...], w1_ref[:, cols], preferred_element_type=F32), 0.0)
        a_ref[:, cols] = (a * a).astype(BF16)

    def down(cols):
        emit(cols, jnp.dot(a_ref[...], w2_ref[:, cols], preferred_element_type=F32))

    slots = [(xn_ref, up, slice(c * FF_COLS, (c + 1) * FF_COLS)) for c in range(d_ff // FF_COLS)]
    slots += [(a_ref, down, slice(c * DOWN_COLS, (c + 1) * DOWN_COLS)) for c in range(d // DOWN_COLS)]
    assert len(side) <= len(slots) - 2
    results = []
    for i, (lhs_ref, run, cols) in enumerate(slots):
        if i >= 2 and len(results) > i - 2:
            _pace(lhs_ref, results[i - 2])
        run(cols)
        if side:
            results.append(side.pop(0)())


def _pool_project(delta, wp_ref, ps_ref):
    ys = [jnp.dot(dl, wp_ref[g].astype(BF16), preferred_element_type=F32) for g, dl in enumerate(delta)]
    return jnp.concatenate(ys, axis=1) * ps_ref[...]


def _sgu_in(h, win_ref, bin_ref, z_ref):
    for c in range(win_ref.shape[1] // FF_COLS):
        cols = slice(c * FF_COLS, (c + 1) * FF_COLS)
        z = jnp.dot(h, win_ref[:, cols], preferred_element_type=F32) + bin_ref[:, cols]
        z_ref[:, cols] = _gelu(z)


def _ada_rows(c_ref, w_ref, b_row):
    s = jax.nn.silu(c_ref[...]).astype(BF16)
    return jnp.dot(s, w_ref[...].astype(BF16), preferred_element_type=F32) + b_row


def _first_layer_prep_kernel(layer, c_ref, w_ref, b_ref, *refs):
    n_cast = (len(refs) - 1) // 2
    o_ref = refs[n_cast]
    o_ref[...] = _ada_rows(c_ref, w_ref, b_ref[layer:layer + 1, :])
    for src, dst in zip(refs[:n_cast], refs[n_cast + 1:]):
        dst[...] = src[...].astype(BF16)


def _first_layer_prep(c_all, w_ada, b_ada, layer, weights):
    depth, d, n = w_ada.shape
    rows = c_all.shape[0]
    n_steps = n // ADA_COLS
    cast_in, cast_out, cast_shapes = [], [], []
    for w in weights:
        r = w.shape[1] // n_steps
        assert w.shape[1] % n_steps == 0 and r % 16 == 0
        cast_in.append(pl.BlockSpec((None, r, w.shape[2]), lambda j: (layer, j, 0)))
        cast_out.append(pl.BlockSpec((None, r, w.shape[2]), lambda j: (0, j, 0)))
        cast_shapes.append(jax.ShapeDtypeStruct((1,) + w.shape[1:], BF16))
    return pl.pallas_call(
        functools.partial(_first_layer_prep_kernel, layer),
        grid=(n_steps,),
        in_specs=[
            pl.BlockSpec((rows, d), lambda j: (0, 0)),
            pl.BlockSpec((None, d, ADA_COLS), lambda j: (layer, 0, j)),
            pl.BlockSpec((depth, ADA_COLS), lambda j: (0, j)),
        ] + cast_in,
        out_specs=[pl.BlockSpec((rows, ADA_COLS), lambda j: (0, j))] + cast_out,
        out_shape=[jax.ShapeDtypeStruct((rows, n), F32)] + cast_shapes,
        compiler_params=pltpu.CompilerParams(
            dimension_semantics=("arbitrary",), vmem_limit_bytes=VMEM_LIMIT),
        name="first_layer_prep",
    )(c_all, w_ada, b_ada, *weights)


def _pipelined_steps(init, mixer_pieces, mlp):
    s = pl.program_id(0)
    n_tiles = pl.num_programs(0) - 1

    @pl.when(s == 0)
    def _():
        init()
        for piece in mixer_pieces(s):
            piece()

    @pl.when(s > 0)
    def _():
        mlp(s - 1, mixer_pieces(jnp.minimum(s, n_tiles - 1)))


def _handoff(xn_next, x1_next, xn_cur, x1_cur):
    xn_cur[...] = xn_next[...]
    x1_cur[...] = x1_next[...]


def _pipelined_tile_maps(n_seq, tiles_per_seq):
    n_tiles = n_seq * tiles_per_seq

    def mixer_tile(s):
        t = jnp.minimum(s, n_tiles - 1)
        return t // tiles_per_seq, t % tiles_per_seq, 0

    def mlp_tile(s):
        t = jnp.maximum(s - 1, 0)
        return t // tiles_per_seq, t % tiles_per_seq, 0

    return mixer_tile, mlp_tile, n_tiles + 1


def _pipeline_scratch(tm, d, d_ff):
    return [pltpu.VMEM((tm, d), BF16), pltpu.VMEM((tm, d), F32), pltpu.VMEM((tm, d), BF16),
            pltpu.VMEM((tm, d), F32), pltpu.VMEM((tm, d_ff), BF16)]


def _layer0_prompt_kernel(tiles_per_seq, next_layer, x_ref, mod_ref, ng_ref, wp_ref, ps_ref, w1_ref, w2_ref,
                          c_ref, wada_ref, bada_ref, *rest):
    n_cast = (len(rest) - 9) // 2
    cast_src = rest[:n_cast]
    o_ref, tail_ref, mod_next_ref = rest[n_cast:n_cast + 3]
    cast_dst = rest[n_cast + 3:2 * n_cast + 3]
    hh_ref, xn_next, x1_next, xn_cur, x1_cur, a_ref = rest[2 * n_cast + 3:]
    tm, d = x_ref.shape
    pg = d // len(POOL_WINDOWS)

    def next_layer_prep():
        mod_next = _ada_rows(c_ref, wada_ref, bada_ref[next_layer:next_layer + 1, :])
        mod_next_ref[...] = mod_next
        for src, dst in zip(cast_src, cast_dst):
            dst[...] = src[...].astype(BF16)
        return mod_next

    def mixer_pieces(tile):
        b, j = tile // tiles_per_seq, tile % tiles_per_seq
        sh1, sc1, gt1, sh2, sc2, _ = _split_mod(mod_ref[pl.ds(b, 1), :], d)
        first = j == 0

        def norm():
            h = _modulate(x_ref[...], ng_ref[0:1, :], sh1, sc1)
            hh_ref[0:HIST] = jnp.where(first, 0.0, hh_ref[tm:tm + HIST])
            hh_ref[HIST:] = h
            for k in range(POOL_BUF):
                tail_ref[k, pl.ds(b, 1), :] = h[tm - POOL_BUF + k:tm - POOL_BUF + k + 1]
            return h

        def group(g):
            w = POOL_WINDOWS[g]
            cols = slice(g * pg, (g + 1) * pg)
            s = hh_ref[:, cols]
            span = 1
            while span < w:
                s = s + pltpu.roll(s, span, axis=0)
                span *= 2
            s = s[HIST:]
            r = lax.broadcasted_iota(jnp.int32, (HIST, pg), 0)
            cnt = jnp.where(first, jnp.minimum(w, r + 1), w).astype(F32)
            m = jnp.concatenate([s[:HIST] / cnt, s[HIST:] * (1.0 / w)], axis=0)
            delta = (m - hh_ref[HIST:, cols]).astype(BF16)
            y = jnp.dot(delta, wp_ref[g].astype(BF16), preferred_element_type=F32) * ps_ref[:, cols]
            x1 = x_ref[:, cols] + gt1[:, cols] * y
            x1_next[:, cols] = x1
            return x1

        def norm2():
            xn = _modulate(x1_next[...], ng_ref[1:2, :], sh2, sc2)
            xn_next[...] = xn.astype(BF16)
            return xn

        return ([norm] + [functools.partial(group, g) for g in range(len(POOL_WINDOWS))]
                + [norm2, next_layer_prep])

    def mlp(tile, side_work):
        gt2 = mod_ref[pl.ds(tile // tiles_per_seq, 1), (N_MOD - 1) * d:]
        _handoff(xn_next, x1_next, xn_cur, x1_cur)

        def emit(cols, y):
            o_ref[:, cols] = x1_cur[:, cols] + gt2[:, cols] * y

        _channel_mlp_with_side_work(xn_cur, w1_ref, w2_ref, a_ref, side_work, emit)

    def init():
        hh_ref[...] = jnp.zeros_like(hh_ref)

    _pipelined_steps(init, mixer_pieces, mlp)


def _layer0_sample_kernel(seq_rows, x_ref, buf_ref, mod_ref, ng_ref, wp_ref, ps_ref, w1_ref, w2_ref,
                          o_ref, pool_ref, a_ref):
    n_seq, _, d = x_ref.shape
    sh1, sc1, gt1, sh2, sc2, gt2 = _split_mod(mod_ref[...], d)

    xt = jnp.swapaxes(x_ref[...], 0, 1)
    xs = [xt[t] for t in range(seq_rows)]
    hs = [_modulate(x, ng_ref[0:1, :], sh1, sc1) for x in xs]
    rows = [buf_ref[k] for k in range(POOL_BUF)] + hs
    for k in range(POOL_BUF):
        pool_ref[k] = rows[len(rows) - POOL_BUF + k]

    pg = d // len(POOL_WINDOWS)
    delta = []
    for g, w in enumerate(POOL_WINDOWS):
        cols = slice(g * pg, (g + 1) * pg)
        sums = {i: rows[i][:, cols] for i in range(len(rows))}
        span = 1
        while span < w:
            first = POOL_BUF - (w - 2 * span)
            sums = {i: sums[i] + sums[i - span] for i in range(first, len(rows))}
            span *= 2
        delta.append(jnp.concatenate(
            [(sums[POOL_BUF + t] * (1.0 / w) - hs[t][:, cols]).astype(BF16) for t in range(seq_rows)], axis=0))
    y = _pool_project(delta, wp_ref, ps_ref)
    x1 = [xs[t] + gt1 * y[t * n_seq:(t + 1) * n_seq] for t in range(seq_rows)]

    xn = jnp.concatenate([_modulate(x, ng_ref[1:2, :], sh2, sc2).astype(BF16) for x in x1], axis=0)
    y = _channel_mlp(xn, w1_ref, w2_ref, a_ref)
    out = [x1[t] + gt2 * y[t * n_seq:(t + 1) * n_seq] for t in range(seq_rows)]
    o_ref[...] = jnp.swapaxes(jnp.stack(out, axis=0), 0, 1)


def _layer0_prompt(x, mod, n_lead, layer, ng, wp, ps, w1, w2, c_all, w_ada, b_ada, next_weights):
    n_seq, seq, d = x.shape
    d_ff = w1.shape[2]
    tm = TILE_ROWS
    mixer_tile, mlp_tile, n_steps = _pipelined_tile_maps(n_seq, seq // tm)
    n_tiles = n_steps - 1
    n_mod = w_ada.shape[2]
    assert n_mod % ADA_PREP_COLS == 0 and n_mod // ADA_PREP_COLS <= n_tiles
    ada_block = lambda s: jnp.minimum(s, n_mod // ADA_PREP_COLS - 1)
    cast_in, cast_out, cast_shapes = [], [], []
    for w, l in next_weights:
        rows = w.shape[1] // n_tiles
        assert w.shape[1] % n_tiles == 0 and rows % 16 == 0
        cast_in.append(pl.BlockSpec((None, rows, w.shape[2]),
                                    lambda s, l=l: (l, jnp.minimum(s, n_tiles - 1), 0)))
        cast_out.append(pl.BlockSpec((None, rows, w.shape[2]), lambda s: (0, jnp.minimum(s, n_tiles - 1), 0)))
        cast_shapes.append(jax.ShapeDtypeStruct((1,) + w.shape[1:], BF16))
    return pl.pallas_call(
        functools.partial(_layer0_prompt_kernel, seq // tm, layer + 1),
        grid=(n_steps,),
        in_specs=[
            pl.BlockSpec((None, tm, d), mixer_tile),
            pl.BlockSpec((n_seq, N_MOD * d), lambda s: (n_lead // n_seq, 0)),
            _resident(ng.shape, layer), _resident(wp.shape, 0), _resident(ps.shape, 0),
            _resident(w1.shape, 0), _resident(w2.shape, 0),
            pl.BlockSpec(c_all.shape, lambda s: (0, 0)),
            pl.BlockSpec((None, d, ADA_PREP_COLS), lambda s: (layer + 1, 0, ada_block(s))),
            pl.BlockSpec((b_ada.shape[0], ADA_PREP_COLS), lambda s: (0, ada_block(s))),
        ] + cast_in,
        out_specs=[
            pl.BlockSpec((None, tm, d), mlp_tile),
            pl.BlockSpec((POOL_BUF, n_seq, d), lambda s: (0, 0, 0)),
            pl.BlockSpec((c_all.shape[0], ADA_PREP_COLS), lambda s: (0, ada_block(s))),
        ] + cast_out,
        out_shape=[
            jax.ShapeDtypeStruct((n_seq, seq, d), F32),
            jax.ShapeDtypeStruct((POOL_BUF, n_seq, d), F32),
            jax.ShapeDtypeStruct((c_all.shape[0], n_mod), F32),
        ] + cast_shapes,
        scratch_shapes=[pltpu.VMEM((HIST + tm, d), F32)] + _pipeline_scratch(tm, d, d_ff),
        compiler_params=pltpu.CompilerParams(
            dimension_semantics=("arbitrary",), vmem_limit_bytes=VMEM_LIMIT),
        name="layer0_prompt",
    )(x, mod, ng, wp, ps, w1, w2, c_all, w_ada, b_ada, *[w for w, _ in next_weights])


def _layer0_sample(x, buf, mod, layer, ng, wp, ps, w1, w2):
    n_seq, seq_rows, d = x.shape
    d_ff = w1.shape[2]
    tb = SAMPLE_SEQS
    tm = tb * seq_rows
    return pl.pallas_call(
        functools.partial(_layer0_sample_kernel, seq_rows),
        grid=(n_seq // tb,),
        in_specs=[
            pl.BlockSpec((tb, seq_rows, d), lambda i: (i, 0, 0)),
            pl.BlockSpec((POOL_BUF, tb, d), lambda i: (0, i, 0)),
            pl.BlockSpec((tb, N_MOD * d), lambda i: (i, 0)),
            _resident(ng.shape, layer), _resident(wp.shape, 0), _resident(ps.shape, 0),
            _resident(w1.shape, 0), _resident(w2.shape, 0),
        ],
        out_specs=[pl.BlockSpec((tb, seq_rows, d), lambda i: (i, 0, 0)),
                   pl.BlockSpec((POOL_BUF, tb, d), lambda i: (0, i, 0))],
        out_shape=[jax.ShapeDtypeStruct(x.shape, F32), jax.ShapeDtypeStruct(buf.shape, F32)],
        scratch_shapes=[pltpu.VMEM((tm, d_ff), BF16)],
        compiler_params=pltpu.CompilerParams(
            dimension_semantics=("arbitrary",), vmem_limit_bytes=VMEM_LIMIT),
        name="layer0_sample",
    )(x, buf, mod, ng, wp, ps, w1, w2)


def _layer1_prompt_kernel(tiles_per_seq, x_ref, mod_ref, ng_ref, win_ref, bin_ref, lng_ref, lnb_ref,
                          wsp_ref, bsp_ref, wout_ref, w1_ref, w2_ref, fg_ref, o_ref, v_ref,
                          hb_ref, z_ref, vnb_ref, gate_ref, xn_next, x1_next, xn_cur, x1_cur, a_ref):
    tm, d = x_ref.shape
    ds = wout_ref.shape[0]
    v_rows = v_ref.shape[0]
    gs = ds // N_SGU_GROUPS

    def mixer_pieces(tile):
        sh1, sc1, gt1, sh2, sc2, _ = _split_mod(mod_ref[pl.ds(tile // tiles_per_seq, 1), :], d)

        def norm():
            h = _modulate(x_ref[...], ng_ref[0:1, :], sh1, sc1)
            hb_ref[...] = h.astype(BF16)
            return h

        def sgu_in(c):
            cols = slice(c * FF_COLS, (c + 1) * FF_COLS)
            z = jnp.dot(hb_ref[...], win_ref[:, cols], preferred_element_type=F32) + bin_ref[:, cols]
            z = _gelu(z)
            z_ref[:, cols] = z
            return z

        def norm_v():
            vn = _layer_norm(z_ref[:, ds:], lng_ref[...], lnb_ref[...])
            v_ref[...] = vn[tm - v_rows:]
            vnb_ref[...] = vn.astype(BF16)
            return vn

        def spatial():
            r = lax.broadcasted_iota(jnp.int32, (CHUNK, CHUNK), 0)
            c = lax.broadcasted_iota(jnp.int32, (CHUNK, CHUNK), 1)
            for g in range(N_SGU_GROUPS):
                wt = jnp.where(c <= r, wsp_ref[g], 0.0).astype(BF16)
                cols = slice(g * gs, (g + 1) * gs)
                for ci in range(tm // CHUNK):
                    rows = slice(ci * CHUNK, (ci + 1) * CHUNK)
                    mixed = jnp.dot(wt, vnb_ref[rows, cols], preferred_element_type=F32) + bsp_ref[:, cols]
                    gate_ref[rows, cols] = (z_ref[rows, cols] * mixed).astype(BF16)
            return mixed

        def sgu_out():
            x1 = x_ref[...] + gt1 * jnp.dot(gate_ref[...], wout_ref[...], preferred_element_type=F32)
            x1_next[...] = x1
            xn = _modulate(x1, ng_ref[1:2, :], sh2, sc2)
            xn_next[...] = xn.astype(BF16)
            return xn

        return ([norm] + [functools.partial(sgu_in, c) for c in range(2 * ds // FF_COLS)]
                + [norm_v, spatial, sgu_out])

    def mlp(tile, side_work):
        gt2 = mod_ref[pl.ds(tile // tiles_per_seq, 1), (N_MOD - 1) * d:]
        _handoff(xn_next, x1_next, xn_cur, x1_cur)

        def emit(cols, y):
            o_ref[:, cols] = x1_cur[:, cols] + gt2[:, cols] * y

        _channel_mlp_with_side_work(xn_cur, w1_ref, w2_ref, a_ref, side_work, emit)
        o_ref[...] = _rms_norm(o_ref[...], fg_ref[...])

    _pipelined_steps(lambda: None, mixer_pieces, mlp)


def _layer1_sample_kernel(seq_rows, wsp_ref, x_ref, mod_ref, ng_ref, win_ref, bin_ref, lng_ref, lnb_ref,
                          bsp_ref, wout_ref, w1_ref, w2_ref, fg_ref, o_ref, v_ref, z_ref, a_ref):
    n_seq, _, d = x_ref.shape
    ds = wout_ref.shape[0]
    sh1, sc1, gt1, sh2, sc2, gt2 = _split_mod(mod_ref[...], d)

    xt = jnp.swapaxes(x_ref[...], 0, 1)
    xs = [xt[t] for t in range(seq_rows)]
    h = jnp.concatenate([_modulate(x, ng_ref[0:1, :], sh1, sc1).astype(BF16) for x in xs], axis=0)
    _sgu_in(h, win_ref, bin_ref, z_ref)
    vn = _layer_norm(z_ref[:, ds:], lng_ref[...], lnb_ref[...])
    vs = [vn[t * n_seq:(t + 1) * n_seq] for t in range(seq_rows)]
    v_ref[...] = jnp.swapaxes(vn.reshape(seq_rows, n_seq, ds), 0, 1)

    gs = ds // N_SGU_GROUPS
    gate = []
    for t in range(seq_rows):
        parts = []
        for g in range(N_SGU_GROUPS):
            cols = slice(g * gs, (g + 1) * gs)
            mixed = bsp_ref[t:t + 1, cols]
            for s in range(t + 1):
                mixed = mixed + wsp_ref[(g * seq_rows + t) * seq_rows + s] * vs[s][:, cols]
            parts.append(mixed)
        gate.append((z_ref[t * n_seq:(t + 1) * n_seq, :ds] * jnp.concatenate(parts, axis=1)).astype(BF16))
    y = jnp.dot(jnp.concatenate(gate, axis=0), wout_ref[...], preferred_element_type=F32)
    x1 = [xs[t] + gt1 * y[t * n_seq:(t + 1) * n_seq] for t in range(seq_rows)]

    xn = jnp.concatenate([_modulate(x, ng_ref[1:2, :], sh2, sc2).astype(BF16) for x in x1], axis=0)
    y = _channel_mlp(xn, w1_ref, w2_ref, a_ref)
    out = [_rms_norm(x1[t] + gt2 * y[t * n_seq:(t + 1) * n_seq], fg_ref[...]) for t in range(seq_rows)]
    o_ref[...] = jnp.swapaxes(jnp.stack(out, axis=0), 0, 1)


def _layer1_weight_specs(layer, ng, win, b_in, lng, lnb, bias, wout, w1, w2, fg):
    return [_resident(ng.shape, layer), _resident(win.shape, 0), _resident(b_in.shape, 0),
            _resident(lng.shape, 0), _resident(lnb.shape, 0), _resident(bias.shape, 0),
            _resident(wout.shape, 0), _resident(w1.shape, 0), _resident(w2.shape, 0),
            _resident(fg.shape, 0)]


def _layer1_prompt(x, mod, n_lead, layer, ng, win, b_in, lng, lnb, wsp, bias, wout, w1, w2, fg):
    n_seq, seq, d = x.shape
    ds = wout.shape[1]
    d_ff = w1.shape[2]
    tm = TILE_ROWS
    v_rows = seq - CHUNK * ((seq - 1) // CHUNK)
    specs = _layer1_weight_specs(layer, ng, win, b_in, lng, lnb, bias, wout, w1, w2, fg)
    mixer_tile, mlp_tile, n_steps = _pipelined_tile_maps(n_seq, seq // tm)
    return pl.pallas_call(
        functools.partial(_layer1_prompt_kernel, seq // tm),
        grid=(n_steps,),
        in_specs=[
            pl.BlockSpec((None, tm, d), mixer_tile),
            pl.BlockSpec((n_seq, N_MOD * d), lambda s: (n_lead // n_seq, 0)),
        ] + specs[:5] + [_resident(wsp.shape, 0)] + specs[5:],
        out_specs=[
            pl.BlockSpec((None, tm, d), mlp_tile),
            pl.BlockSpec((None, v_rows, ds), lambda s: (mixer_tile(s)[0], 0, 0)),
        ],
        out_shape=[
            jax.ShapeDtypeStruct((n_seq, seq, d), F32),
            jax.ShapeDtypeStruct((n_seq, v_rows, ds), F32),
        ],
        scratch_shapes=[pltpu.VMEM((tm, d), BF16), pltpu.VMEM((tm, 2 * ds), F32), pltpu.VMEM((tm, ds), BF16),
                        pltpu.VMEM((tm, ds), BF16)] + _pipeline_scratch(tm, d, d_ff),
        compiler_params=pltpu.CompilerParams(
            dimension_semantics=("arbitrary",), vmem_limit_bytes=VMEM_LIMIT),
        name="layer1_prompt",
    )(x, mod, ng, win, b_in, lng, lnb, wsp, bias, wout, w1, w2, fg)


def _layer1_sample(x, mod, layer, wsp_corner, ng, win, b_in, lng, lnb, bias, wout, w1, w2, fg):
    n_seq, seq_rows, d = x.shape
    ds = wout.shape[1]
    d_ff = w1.shape[2]
    tb = SAMPLE_SEQS
    tm = tb * seq_rows
    return pl.pallas_call(
        functools.partial(_layer1_sample_kernel, seq_rows),
        grid=(n_seq // tb,),
        in_specs=[
            pl.BlockSpec(memory_space=pltpu.SMEM),
            pl.BlockSpec((tb, seq_rows, d), lambda i: (i, 0, 0)),
            pl.BlockSpec((tb, N_MOD * d), lambda i: (i, 0)),
        ] + _layer1_weight_specs(layer, ng, win, b_in, lng, lnb, bias, wout, w1, w2, fg),
        out_specs=[pl.BlockSpec((tb, seq_rows, d), lambda i: (i, 0, 0)),
                   pl.BlockSpec((tb, seq_rows, ds), lambda i: (i, 0, 0))],
        out_shape=[jax.ShapeDtypeStruct(x.shape, F32), jax.ShapeDtypeStruct((n_seq, seq_rows, ds), F32)],
        scratch_shapes=[pltpu.VMEM((tm, 2 * ds), F32), pltpu.VMEM((tm, d_ff), BF16)],
        compiler_params=pltpu.CompilerParams(
            dimension_semantics=("arbitrary",), vmem_limit_bytes=VMEM_LIMIT),
        name="layer1_sample",
    )(wsp_corner, x, mod, ng, win, b_in, lng, lnb, bias, wout, w1, w2, fg)


def kernel(x_prompt, x_sample, c_prompt, c_sample, state_pool, norm_g, w_ada, b_ada, w_pool, pool_scale,
           sgu_w_in, sgu_b_in, sgu_ln_g, sgu_ln_b, sgu_w_sp, sgu_b_sp, sgu_w_out, mlp_w1, mlp_w2, final_g):
    n_p, seq, d = x_prompt.shape
    n_s, dec_seq, _ = x_sample.shape
    assert norm_g.shape[0] == 2 and state_pool.shape[0] == 1 and sgu_w_in.shape[0] == 1
    assert state_pool.shape[2] == POOL_BUF and dec_seq <= POOL_BUF and dec_seq <= CHUNK
    assert seq % TILE_ROWS == 0 and TILE_ROWS % CHUNK == 0 and TILE_ROWS >= HIST
    assert n_s % SAMPLE_SEQS == 0 and n_s % n_p == 0

    c_all = jnp.concatenate([c_sample, c_prompt], axis=0)
    mod0, w1_0, w2_0 = _first_layer_prep(c_all, w_ada, b_ada, 0, [mlp_w1, mlp_w2])

    l0 = (norm_g, w_pool, pool_scale, w1_0, w2_0)
    xp, pool_p, mod1, w_in, w_out, w1, w2 = _layer0_prompt(
        x_prompt, mod0, n_s, 0, *l0, c_all, w_ada, b_ada,
        [(sgu_w_in, 0), (sgu_w_out, 0), (mlp_w1, 1), (mlp_w2, 1)])
    buf = jnp.transpose(state_pool[0], (1, 0, 2))
    xs, pool_s = _layer0_sample(x_sample, buf, mod0, 0, *l0)

    ds = sgu_w_out.shape[1]
    bias = jnp.repeat(sgu_b_sp[0].T, ds // N_SGU_GROUPS, axis=1)[None]
    l1 = (norm_g, w_in, sgu_b_in, sgu_ln_g, sgu_ln_b, bias, w_out, w1, w2, final_g[None])
    yp, v_p = _layer1_prompt(xp, mod1, n_s, 1, *l1[:5], sgu_w_sp, *l1[5:])
    corner = sgu_w_sp[0][:, :dec_seq, :dec_seq].reshape(-1)
    ys, v_s = _layer1_sample(xs, mod1, 1, corner, *l1)

    return (yp, ys,
            jnp.transpose(pool_p, (1, 0, 2))[None], jnp.transpose(pool_s, (1, 0, 2))[None],
            v_p[None], v_s[None])
```

```python
import functools

import jax
import jax.numpy as jnp
from jax import lax
from jax.experimental import pallas as pl
from jax.experimental.pallas import tpu as pltpu

F32 = jnp.float32
BF16 = jnp.bfloat16

POOL_WINDOWS = (2, 4, 8, 16)
POOL_BUF = max(POOL_WINDOWS) - 1
HIST = 16
CHUNK = 128
N_SGU_GROUPS = 4
N_MOD = 6
EPS = 1e-6

TILE_ROWS = 512
SAMPLE_SEQS = 32
FF_COLS = 512
DOWN_COLS = 512
ADA_COLS = 768
ADA_PREP_COLS = 256
VMEM_LIMIT = 62 * 1024 * 1024


def _rms_norm(x, g):
    ms = jnp.mean(x * x, axis=-1, keepdims=True)
    return x * lax.rsqrt(ms + EPS) * g


def _modulate(x, g, shift, scale):
    return _rms_norm(x, g) * (1 + scale) + shift


def _gelu(z):
    return 0.5 * z * (1 + lax.erf(z * (0.5 ** 0.5)))


def _layer_norm(v, g, b):
    mu = jnp.mean(v, axis=-1, keepdims=True)
    vc = v - mu
    var = jnp.mean(vc * vc, axis=-1, keepdims=True)
    return vc * lax.rsqrt(var + EPS) * g + b


def _resident(shape, layer):
    if len(shape) == 2:
        assert shape[0] == 1 and layer == 0
        return pl.BlockSpec(tuple(shape), lambda *_: (0, 0), pipeline_mode=pl.Buffered(1))
    return pl.BlockSpec((None,) + tuple(shape[1:]), lambda *_: (layer,) + (0,) * (len(shape) - 1),
                        pipeline_mode=pl.Buffered(1))


def _split_mod(mod, d):
    return [mod[:, k * d:(k + 1) * d] for k in range(N_MOD)]


def _channel_mlp(xn, w1_ref, w2_ref, a_ref):
    d_ff = w1_ref.shape[1]
    for c in range(d_ff // FF_COLS):
        cols = slice(c * FF_COLS, (c + 1) * FF_COLS)
        a = jnp.maximum(jnp.dot(xn, w1_ref[:, cols], preferred_element_type=F32), 0.0)
        a_ref[:, cols] = (a * a).astype(BF16)
    return jnp.dot(a_ref[...], w2_ref[...], preferred_element_type=F32)


def _pace(lhs_ref, v):
    bits = pltpu.bitcast(v[0:8, 0:128].astype(F32), jnp.int32)
    zero = lax.shift_right_logical(lax.shift_right_logical(bits, 16), 16).astype(F32)
    rows = 32 // lhs_ref.dtype.itemsize
    zero = jnp.concatenate([zero] * (rows // 8), axis=0).astype(lhs_ref.dtype)
    lhs_ref[0:rows, 0:128] = lhs_ref[0:rows, 0:128] + zero


def _channel_mlp_with_side_work(xn_ref, w1_ref, w2_ref, a_ref, side_work, emit):
    side = list(side_work)
    d, d_ff = w1_ref.shape

    def up(cols):
        a = jnp.maximum(jnp.dot(xn_ref[...], w1_ref[:, cols], preferred_element_type=F32), 0.0)
        a_ref[:, cols] = (a * a).astype(BF16)

    def down(cols):
        emit(cols, jnp.dot(a_ref[...], w2_ref[:, cols], preferred_element_type=F32))

    slots = [(xn_ref, up, slice(c * FF_COLS, (c + 1) * FF_COLS)) for c in range(d_ff // FF_COLS)]
    slots += [(a_ref, down, slice(c * DOWN_COLS, (c + 1) * DOWN_COLS)) for c in range(d // DOWN_COLS)]
    assert len(side) <= len(slots) - 2
    results = []
    for i, (lhs_ref, run, cols) in enumerate(slots):
        if i >= 2 and len(results) > i - 2:
            _pace(lhs_ref, results[i - 2])
        run(cols)
        if side:
            results.append(side.pop(0)())


def _pool_project(delta, wp_ref, ps_ref):
    ys = [jnp.dot(dl, wp_ref[g].astype(BF16), preferred_element_type=F32) for g, dl in enumerate(delta)]
    return jnp.concatenate(ys, axis=1) * ps_ref[...]


def _sgu_in(h, win_ref, bin_ref, z_ref):
    for c in range(win_ref.shape[1] // FF_COLS):
        cols = slice(c * FF_COLS, (c + 1) * FF_COLS)
        z = jnp.dot(h, win_ref[:, cols], preferred_element_type=F32) + bin_ref[:, cols]
        z_ref[:, cols] = _gelu(z)


def _ada_rows(c_ref, w_ref, b_row):
    s = jax.nn.silu(c_ref[...]).astype(BF16)
    return jnp.dot(s, w_ref[...].astype(BF16), preferred_element_type=F32) + b_row


def _first_layer_prep_kernel(layer, c_ref, w_ref, b_ref, *refs):
    n_cast = (len(refs) - 1) // 2
    o_ref = refs[n_cast]
    o_ref[...] = _ada_rows(c_ref, w_ref, b_ref[layer:layer + 1, :])
    for src, dst in zip(refs[:n_cast], refs[n_cast + 1:]):
        dst[...] = src[...].astype(BF16)


def _first_layer_prep(c_all, w_ada, b_ada, layer, weights):
    depth, d, n = w_ada.shape
    rows = c_all.shape[0]
    n_steps = n // ADA_COLS
    cast_in, cast_out, cast_shapes = [], [], []
    for w in weights:
        r = w.shape[1] // n_steps
        assert w.shape[1] % n_steps == 0 and r % 16 == 0
        cast_in.append(pl.BlockSpec((None, r, w.shape[2]), lambda j: (layer, j, 0)))
        cast_out.append(pl.BlockSpec((None, r, w.shape[2]), lambda j: (0, j, 0)))
        cast_shapes.append(jax.ShapeDtypeStruct((1,) + w.shape[1:], BF16))
    return pl.pallas_call(
        functools.partial(_first_layer_prep_kernel, layer),
        grid=(n_steps,),
        in_specs=[
            pl.BlockSpec((rows, d), lambda j: (0, 0)),
            pl.BlockSpec((None, d, ADA_COLS), lambda j: (layer, 0, j)),
            pl.BlockSpec((depth, ADA_COLS), lambda j: (0, j)),
        ] + cast_in,
        out_specs=[pl.BlockSpec((rows, ADA_COLS), lambda j: (0, j))] + cast_out,
        out_shape=[jax.ShapeDtypeStruct((rows, n), F32)] + cast_shapes,
        compiler_params=pltpu.CompilerParams(
            dimension_semantics=("arbitrary",), vmem_limit_bytes=VMEM_LIMIT),
        name="first_layer_prep",
    )(c_all, w_ada, b_ada, *weights)


def _pipelined_steps(n_tiles, init, mixer_pieces, mlp, sample_tile):
    s = pl.program_id(0)

    @pl.when(s == 0)
    def _():
        init()
        for piece in mixer_pieces(s):
            piece()

    @pl.when((s > 0) & (s <= n_tiles))
    def _():
        mlp(s - 1, mixer_pieces(jnp.minimum(s, n_tiles - 1)))

    @pl.when(s > n_tiles)
    def _():
        sample_tile()


def _handoff(xn_next, x1_next, xn_cur, x1_cur):
    xn_cur[...] = xn_next[...]
    x1_cur[...] = x1_next[...]


def _pipelined_tile_maps(n_seq, tiles_per_seq, n_sample_tiles):
    n_tiles = n_seq * tiles_per_seq

    def mixer_tile(s):
        t = jnp.minimum(s, n_tiles - 1)
        return t // tiles_per_seq, t % tiles_per_seq, 0

    def mlp_tile(s):
        t = jnp.clip(s - 1, 0, n_tiles - 1)
        return t // tiles_per_seq, t % tiles_per_seq, 0

    def sample_tile(s):
        return jnp.clip(s - n_tiles - 1, 0, n_sample_tiles - 1)

    return mixer_tile, mlp_tile, sample_tile, n_tiles + 1 + n_sample_tiles


def _pipeline_scratch(tm, d, d_ff):
    return [pltpu.VMEM((tm, d), BF16), pltpu.VMEM((tm, d), F32), pltpu.VMEM((tm, d), BF16),
            pltpu.VMEM((tm, d), F32), pltpu.VMEM((tm, d_ff), BF16)]


def _layer0_kernel(tiles_per_seq, n_tiles, next_layer, x_ref, mod_ref, ng_ref, wp_ref, ps_ref, w1_ref, w2_ref,
                   c_ref, wada_ref, bada_ref, xs_ref, bufs_ref, mods_ref, *rest):
    n_cast = (len(rest) - 11) // 2
    cast_src = rest[:n_cast]
    o_ref, tail_ref, mod_next_ref = rest[n_cast:n_cast + 3]
    cast_dst = rest[n_cast + 3:2 * n_cast + 3]
    os_ref, pools_ref = rest[2 * n_cast + 3:2 * n_cast + 5]
    hh_ref, xn_next, x1_next, xn_cur, x1_cur, a_ref = rest[2 * n_cast + 5:]
    tm, d = x_ref.shape
    pg = d // len(POOL_WINDOWS)

    def sample_tile():
        rows = xs_ref.shape[0] * xs_ref.shape[1]
        _layer0_sample_tile(xs_ref, bufs_ref, mods_ref, ng_ref, wp_ref, ps_ref, w1_ref, w2_ref,
                            os_ref, pools_ref, a_ref.at[pl.ds(0, rows)])

    def next_layer_prep():
        mod_next = _ada_rows(c_ref, wada_ref, bada_ref[next_layer:next_layer + 1, :])
        mod_next_ref[...] = mod_next
        for src, dst in zip(cast_src, cast_dst):
            dst[...] = src[...].astype(BF16)
        return mod_next

    def mixer_pieces(tile):
        b, j = tile // tiles_per_seq, tile % tiles_per_seq
        sh1, sc1, gt1, sh2, sc2, _ = _split_mod(mod_ref[pl.ds(b, 1), :], d)
        first = j == 0

        def norm():
            h = _modulate(x_ref[...], ng_ref[0:1, :], sh1, sc1)
            hh_ref[0:HIST] = jnp.where(first, 0.0, hh_ref[tm:tm + HIST])
            hh_ref[HIST:] = h
            for k in range(POOL_BUF):
                tail_ref[k, pl.ds(b, 1), :] = h[tm - POOL_BUF + k:tm - POOL_BUF + k + 1]
            return h

        def group(g):
            w = POOL_WINDOWS[g]
            cols = slice(g * pg, (g + 1) * pg)
            s = hh_ref[:, cols]
            span = 1
            while span < w:
                s = s + pltpu.roll(s, span, axis=0)
                span *= 2
            s = s[HIST:]
            r = lax.broadcasted_iota(jnp.int32, (HIST, pg), 0)
            cnt = jnp.where(first, jnp.minimum(w, r + 1), w).astype(F32)
            m = jnp.concatenate([s[:HIST] / cnt, s[HIST:] * (1.0 / w)], axis=0)
            delta = (m - hh_ref[HIST:, cols]).astype(BF16)
            y = jnp.dot(delta, wp_ref[g].astype(BF16), preferred_element_type=F32) * ps_ref[:, cols]
            x1 = x_ref[:, cols] + gt1[:, cols] * y
            x1_next[:, cols] = x1
            return x1

        def norm2():
            xn = _modulate(x1_next[...], ng_ref[1:2, :], sh2, sc2)
            xn_next[...] = xn.astype(BF16)
            return xn

        return ([norm] + [functools.partial(group, g) for g in range(len(POOL_WINDOWS))]
                + [norm2, next_layer_prep])

    def mlp(tile, side_work):
        gt2 = mod_ref[pl.ds(tile // tiles_per_seq, 1), (N_MOD - 1) * d:]
        _handoff(xn_next, x1_next, xn_cur, x1_cur)

        def emit(cols, y):
            o_ref[:, cols] = x1_cur[:, cols] + gt2[:, cols] * y

        _channel_mlp_with_side_work(xn_cur, w1_ref, w2_ref, a_ref, side_work, emit)

    def init():
        hh_ref[...] = jnp.zeros_like(hh_ref)

    _pipelined_steps(n_tiles, init, mixer_pieces, mlp, sample_tile)


def _layer0_sample_tile(x_ref, buf_ref, mod_ref, ng_ref, wp_ref, ps_ref, w1_ref, w2_ref,
                        o_ref, pool_ref, a_ref):
    n_seq, seq_rows, d = x_ref.shape
    sh1, sc1, gt1, sh2, sc2, gt2 = _split_mod(mod_ref[...], d)

    xt = jnp.swapaxes(x_ref[...], 0, 1)
    xs = [xt[t] for t in range(seq_rows)]
    hs = [_modulate(x, ng_ref[0:1, :], sh1, sc1) for x in xs]
    rows = [buf_ref[k] for k in range(POOL_BUF)] + hs
    for k in range(POOL_BUF):
        pool_ref[k] = rows[len(rows) - POOL_BUF + k]

    pg = d // len(POOL_WINDOWS)
    delta = []
    for g, w in enumerate(POOL_WINDOWS):
        cols = slice(g * pg, (g + 1) * pg)
        sums = {i: rows[i][:, cols] for i in range(len(rows))}
        span = 1
        while span < w:
            first = POOL_BUF - (w - 2 * span)
            sums = {i: sums[i] + sums[i - span] for i in range(first, len(rows))}
            span *= 2
        delta.append(jnp.concatenate(
            [(sums[POOL_BUF + t] * (1.0 / w) - hs[t][:, cols]).astype(BF16) for t in range(seq_rows)], axis=0))
    y = _pool_project(delta, wp_ref, ps_ref)
    x1 = [xs[t] + gt1 * y[t * n_seq:(t + 1) * n_seq] for t in range(seq_rows)]

    xn = jnp.concatenate([_modulate(x, ng_ref[1:2, :], sh2, sc2).astype(BF16) for x in x1], axis=0)
    y = _channel_mlp(xn, w1_ref, w2_ref, a_ref)
    out = [x1[t] + gt2 * y[t * n_seq:(t + 1) * n_seq] for t in range(seq_rows)]
    o_ref[...] = jnp.swapaxes(jnp.stack(out, axis=0), 0, 1)


def _layer0(x, xs, bufs, mod, layer, ng, wp, ps, w1, w2, c_all, w_ada, b_ada, next_weights):
    n_seq, seq, d = x.shape
    n_s, seq_rows, _ = xs.shape
    d_ff = w1.shape[2]
    tm = TILE_ROWS
    tb = SAMPLE_SEQS
    n_tiles = n_seq * (seq // tm)
    mixer_tile, mlp_tile, sample_tile, n_steps = _pipelined_tile_maps(n_seq, seq // tm, n_s // tb)
    assert tb * seq_rows <= tm and n_s % tb == 0 and n_s % n_seq == 0
    n_mod = w_ada.shape[2]
    assert n_mod % ADA_PREP_COLS == 0 and n_mod // ADA_PREP_COLS <= n_tiles
    ada_block = lambda s: jnp.minimum(s, n_mod // ADA_PREP_COLS - 1)
    cast_in, cast_out, cast_shapes = [], [], []
    for w, l in next_weights:
        rows = w.shape[1] // n_tiles
        assert w.shape[1] % n_tiles == 0 and rows % 16 == 0
        cast_in.append(pl.BlockSpec((None, rows, w.shape[2]),
                                    lambda s, l=l: (l, jnp.minimum(s, n_tiles - 1), 0)))
        cast_out.append(pl.BlockSpec((None, rows, w.shape[2]), lambda s: (0, jnp.minimum(s, n_tiles - 1), 0)))
        cast_shapes.append(jax.ShapeDtypeStruct((1,) + w.shape[1:], BF16))
    return pl.pallas_call(
        functools.partial(_layer0_kernel, seq // tm, n_tiles, layer + 1),
        grid=(n_steps,),
        in_specs=[
            pl.BlockSpec((None, tm, d), mixer_tile),
            pl.BlockSpec((n_seq, N_MOD * d), lambda s: (n_s // n_seq, 0)),
            _resident(ng.shape, layer), _resident(wp.shape, 0), _resident(ps.shape, 0),
            _resident(w1.shape, 0), _resident(w2.shape, 0),
            pl.BlockSpec(c_all.shape, lambda s: (0, 0)),
            pl.BlockSpec((None, d, ADA_PREP_COLS), lambda s: (layer + 1, 0, ada_block(s))),
            pl.BlockSpec((b_ada.shape[0], ADA_PREP_COLS), lambda s: (0, ada_block(s))),
            pl.BlockSpec((tb, seq_rows, d), lambda s: (sample_tile(s), 0, 0)),
            pl.BlockSpec((POOL_BUF, tb, d), lambda s: (0, sample_tile(s), 0)),
            pl.BlockSpec((tb, N_MOD * d), lambda s: (sample_tile(s), 0)),
        ] + cast_in,
        out_specs=[
            pl.BlockSpec((None, tm, d), mlp_tile),
            pl.BlockSpec((POOL_BUF, n_seq, d), lambda s: (0, 0, 0)),
            pl.BlockSpec((c_all.shape[0], ADA_PREP_COLS), lambda s: (0, ada_block(s))),
        ] + cast_out + [
            pl.BlockSpec((tb, seq_rows, d), lambda s: (sample_tile(s), 0, 0)),
            pl.BlockSpec((POOL_BUF, tb, d), lambda s: (0, sample_tile(s), 0)),
        ],
        out_shape=[
            jax.ShapeDtypeStruct((n_seq, seq, d), F32),
            jax.ShapeDtypeStruct((POOL_BUF, n_seq, d), F32),
            jax.ShapeDtypeStruct((c_all.shape[0], n_mod), F32),
        ] + cast_shapes + [jax.ShapeDtypeStruct(xs.shape, F32), jax.ShapeDtypeStruct(bufs.shape, F32)],
        scratch_shapes=[pltpu.VMEM((HIST + tm, d), F32)] + _pipeline_scratch(tm, d, d_ff),
        compiler_params=pltpu.CompilerParams(
            dimension_semantics=("arbitrary",), vmem_limit_bytes=VMEM_LIMIT),
        name="layer0",
    )(x, mod, ng, wp, ps, w1, w2, c_all, w_ada, b_ada, xs, bufs, mod, *[w for w, _ in next_weights])


def _layer1_kernel(tiles_per_seq, n_tiles, x_ref, mod_ref, ng_ref, win_ref, bin_ref, lng_ref, lnb_ref,
                   wsp_ref, bsp_ref, wout_ref, w1_ref, w2_ref, fg_ref, corner_ref, xs_ref, mods_ref,
                   o_ref, v_ref, os_ref, vs_ref,
                   hb_ref, z_ref, vnb_ref, gate_ref, xn_next, x1_next, xn_cur, x1_cur, a_ref):
    tm, d = x_ref.shape
    ds = wout_ref.shape[0]
    v_rows = v_ref.shape[0]
    gs = ds // N_SGU_GROUPS

    def sample_tile():
        rows = pl.ds(0, xs_ref.shape[0] * xs_ref.shape[1])
        _layer1_sample_tile(corner_ref, xs_ref, mods_ref, ng_ref, win_ref, bin_ref, lng_ref, lnb_ref,
                            bsp_ref, wout_ref, w1_ref, w2_ref, fg_ref, os_ref, vs_ref,
                            z_ref.at[rows], a_ref.at[rows])

    def mixer_pieces(tile):
        sh1, sc1, gt1, sh2, sc2, _ = _split_mod(mod_ref[pl.ds(tile // tiles_per_seq, 1), :], d)

        def norm():
            h = _modulate(x_ref[...], ng_ref[0:1, :], sh1, sc1)
            hb_ref[...] = h.astype(BF16)
            return h

        def sgu_in(c):
            cols = slice(c * FF_COLS, (c + 1) * FF_COLS)
            z = jnp.dot(hb_ref[...], win_ref[:, cols], preferred_element_type=F32) + bin_ref[:, cols]
            z = _gelu(z)
            z_ref[:, cols] = z
            return z

        def norm_v():
            vn = _layer_norm(z_ref[:, ds:], lng_ref[...], lnb_ref[...])
            v_ref[...] = vn[tm - v_rows:]
            vnb_ref[...] = vn.astype(BF16)
            return vn

        def spatial():
            r = lax.broadcasted_iota(jnp.int32, (CHUNK, CHUNK), 0)
            c = lax.broadcasted_iota(jnp.int32, (CHUNK, CHUNK), 1)
            chunks = [slice(ci * CHUNK, (ci + 1) * CHUNK) for ci in range(tm // CHUNK)]
            for g in range(N_SGU_GROUPS):
                wt = jnp.where(c <= r, wsp_ref[g], 0.0).astype(BF16)
                cols = slice(g * gs, (g + 1) * gs)
                v_all = jnp.concatenate([vnb_ref[rows, cols] for rows in chunks], axis=1)
                mixed_all = jnp.dot(wt, v_all, preferred_element_type=F32)
                for ci, rows in enumerate(chunks):
                    mixed = mixed_all[:, ci * gs:(ci + 1) * gs] + bsp_ref[:, cols]
                    gate_ref[rows, cols] = (z_ref[rows, cols] * mixed).astype(BF16)
            return mixed

        def sgu_out():
            x1 = x_ref[...] + gt1 * jnp.dot(gate_ref[...], wout_ref[...], preferred_element_type=F32)
            x1_next[...] = x1
            xn = _modulate(x1, ng_ref[1:2, :], sh2, sc2)
            xn_next[...] = xn.astype(BF16)
            return xn

        return ([norm] + [functools.partial(sgu_in, c) for c in range(2 * ds // FF_COLS)]
                + [norm_v, spatial, sgu_out])

    def mlp(tile, side_work):
        gt2 = mod_ref[pl.ds(tile // tiles_per_seq, 1), (N_MOD - 1) * d:]
        _handoff(xn_next, x1_next, xn_cur, x1_cur)

        def emit(cols, y):
            o_ref[:, cols] = x1_cur[:, cols] + gt2[:, cols] * y

        _channel_mlp_with_side_work(xn_cur, w1_ref, w2_ref, a_ref, side_work, emit)
        o_ref[...] = _rms_norm(o_ref[...], fg_ref[...])

    _pipelined_steps(n_tiles, lambda: None, mixer_pieces, mlp, sample_tile)


def _layer1_sample_tile(wsp_ref, x_ref, mod_ref, ng_ref, win_ref, bin_ref, lng_ref, lnb_ref,
                        bsp_ref, wout_ref, w1_ref, w2_ref, fg_ref, o_ref, v_ref, z_ref, a_ref):
    n_seq, seq_rows, d = x_ref.shape
    ds = wout_ref.shape[0]
    sh1, sc1, gt1, sh2, sc2, gt2 = _split_mod(mod_ref[...], d)

    xt = jnp.swapaxes(x_ref[...], 0, 1)
    xs = [xt[t] for t in range(seq_rows)]
    h = jnp.concatenate([_modulate(x, ng_ref[0:1, :], sh1, sc1).astype(BF16) for x in xs], axis=0)
    _sgu_in(h, win_ref, bin_ref, z_ref)
    vn = _layer_norm(z_ref[:, ds:], lng_ref[...], lnb_ref[...])
    vs = [vn[t * n_seq:(t + 1) * n_seq] for t in range(seq_rows)]
    v_ref[...] = jnp.swapaxes(vn.reshape(seq_rows, n_seq, ds), 0, 1)

    gs = ds // N_SGU_GROUPS
    gate = []
    for t in range(seq_rows):
        parts = []
        for g in range(N_SGU_GROUPS):
            cols = slice(g * gs, (g + 1) * gs)
            mixed = bsp_ref[t:t + 1, cols]
            for s in range(t + 1):
                mixed = mixed + wsp_ref[(g * seq_rows + t) * seq_rows + s] * vs[s][:, cols]
            parts.append(mixed)
        gate.append((z_ref[t * n_seq:(t + 1) * n_seq, :ds] * jnp.concatenate(parts, axis=1)).astype(BF16))
    y = jnp.dot(jnp.concatenate(gate, axis=0), wout_ref[...], preferred_element_type=F32)
    x1 = [xs[t] + gt1 * y[t * n_seq:(t + 1) * n_seq] for t in range(seq_rows)]

    xn = jnp.concatenate([_modulate(x, ng_ref[1:2, :], sh2, sc2).astype(BF16) for x in x1], axis=0)
    y = _channel_mlp(xn, w1_ref, w2_ref, a_ref)
    out = [_rms_norm(x1[t] + gt2 * y[t * n_seq:(t + 1) * n_seq], fg_ref[...]) for t in range(seq_rows)]
    o_ref[...] = jnp.swapaxes(jnp.stack(out, axis=0), 0, 1)


def _layer1_weight_specs(layer, ng, win, b_in, lng, lnb, bias, wout, w1, w2, fg):
    return [_resident(ng.shape, layer), _resident(win.shape, 0), _resident(b_in.shape, 0),
            _resident(lng.shape, 0), _resident(lnb.shape, 0), _resident(bias.shape, 0),
            _resident(wout.shape, 0), _resident(w1.shape, 0), _resident(w2.shape, 0),
            _resident(fg.shape, 0)]


def _layer1(x, xs, mod, layer, wsp_corner, ng, win, b_in, lng, lnb, wsp, bias, wout, w1, w2, fg):
    n_seq, seq, d = x.shape
    n_s, seq_rows, _ = xs.shape
    ds = wout.shape[1]
    d_ff = w1.shape[2]
    tm = TILE_ROWS
    tb = SAMPLE_SEQS
    n_tiles = n_seq * (seq // tm)
    v_rows = seq - CHUNK * ((seq - 1) // CHUNK)
    specs = _layer1_weight_specs(layer, ng, win, b_in, lng, lnb, bias, wout, w1, w2, fg)
    mixer_tile, mlp_tile, sample_tile, n_steps = _pipelined_tile_maps(n_seq, seq // tm, n_s // tb)
    assert tb * seq_rows <= tm and n_s % tb == 0 and n_s % n_seq == 0
    return pl.pallas_call(
        functools.partial(_layer1_kernel, seq // tm, n_tiles),
        grid=(n_steps,),
        in_specs=[
            pl.BlockSpec((None, tm, d), mixer_tile),
            pl.BlockSpec((n_seq, N_MOD * d), lambda s: (n_s // n_seq, 0)),
        ] + specs[:5] + [_resident(wsp.shape, 0)] + specs[5:] + [
            pl.BlockSpec(memory_space=pltpu.SMEM),
            pl.BlockSpec((tb, seq_rows, d), lambda s: (sample_tile(s), 0, 0)),
            pl.BlockSpec((tb, N_MOD * d), lambda s: (sample_tile(s), 0)),
        ],
        out_specs=[
            pl.BlockSpec((None, tm, d), mlp_tile),
            pl.BlockSpec((None, v_rows, ds), lambda s: (mixer_tile(s)[0], 0, 0)),
            pl.BlockSpec((tb, seq_rows, d), lambda s: (sample_tile(s), 0, 0)),
            pl.BlockSpec((tb, seq_rows, ds), lambda s: (sample_tile(s), 0, 0)),
        ],
        out_shape=[
            jax.ShapeDtypeStruct((n_seq, seq, d), F32),
            jax.ShapeDtypeStruct((n_seq, v_rows, ds), F32),
            jax.ShapeDtypeStruct(xs.shape, F32),
            jax.ShapeDtypeStruct((n_s, seq_rows, ds), F32),
        ],
        scratch_shapes=[pltpu.VMEM((tm, d), BF16), pltpu.VMEM((tm, 2 * ds), F32), pltpu.VMEM((tm, ds), BF16),
                        pltpu.VMEM((tm, ds), BF16)] + _pipeline_scratch(tm, d, d_ff),
        compiler_params=pltpu.CompilerParams(
            dimension_semantics=("arbitrary",), vmem_limit_bytes=VMEM_LIMIT),
        name="layer1",
    )(x, mod, ng, win, b_in, lng, lnb, wsp, bias, wout, w1, w2, fg, wsp_corner, xs, mod)


def kernel(x_prompt, x_sample, c_prompt, c_sample, state_pool, norm_g, w_ada, b_ada, w_pool, pool_scale,
           sgu_w_in, sgu_b_in, sgu_ln_g, sgu_ln_b, sgu_w_sp, sgu_b_sp, sgu_w_out, mlp_w1, mlp_w2, final_g):
    n_p, seq, d = x_prompt.shape
    n_s, dec_seq, _ = x_sample.shape
    assert norm_g.shape[0] == 2 and state_pool.shape[0] == 1 and sgu_w_in.shape[0] == 1
    assert state_pool.shape[2] == POOL_BUF and dec_seq <= POOL_BUF and dec_seq <= CHUNK
    assert seq % TILE_ROWS == 0 and TILE_ROWS % CHUNK == 0 and TILE_ROWS >= HIST
    assert n_s % SAMPLE_SEQS == 0 and n_s % n_p == 0

    c_all = jnp.concatenate([c_sample, c_prompt], axis=0)
    mod0, w1_0, w2_0 = _first_layer_prep(c_all, w_ada, b_ada, 0, [mlp_w1, mlp_w2])

    buf = jnp.transpose(state_pool[0], (1, 0, 2))
    xp, pool_p, mod1, w_in, w_out, w1, w2, xs, pool_s = _layer0(
        x_prompt, x_sample, buf, mod0, 0, norm_g, w_pool, pool_scale, w1_0, w2_0, c_all, w_ada, b_ada,
        [(sgu_w_in, 0), (sgu_w_out, 0), (mlp_w1, 1), (mlp_w2, 1)])

    ds = sgu_w_out.shape[1]
    bias = jnp.repeat(sgu_b_sp[0].T, ds // N_SGU_GROUPS, axis=1)[None]
    corner = sgu_w_sp[0][:, :dec_seq, :dec_seq].reshape(-1)
    yp, v_p, ys, v_s = _layer1(xp, xs, mod1, 1, corner, norm_g, w_in, sgu_b_in, sgu_ln_g, sgu_ln_b,
                               sgu_w_sp, bias, w_out, w1, w2, final_g[None])

    return (yp, ys,
            jnp.transpose(pool_p, (1, 0, 2))[None], jnp.transpose(pool_s, (1, 0, 2))[None],
            v_p[None], v_s[None])
```

```python
import functools

import jax
import jax.numpy as jnp
from jax import lax
from jax.experimental import pallas as pl
from jax.experimental.pallas import tpu as pltpu

F32 = jnp.float32
BF16 = jnp.bfloat16

POOL_WINDOWS = (2, 4, 8, 16)
POOL_BUF = max(POOL_WINDOWS) - 1
HIST = 16
CHUNK = 128
N_SGU_GROUPS = 4
N_MOD = 6
EPS = 1e-6

TILE_ROWS = 512
SAMPLE_SEQS = 32
L1_SAMPLE_SEQS = 64
FF_COLS = 512
DOWN_COLS = 512
ADA_COLS = 768
ADA_PREP_COLS = 256
VMEM_LIMIT = 62 * 1024 * 1024
L1_VMEM_LIMIT = 56 * 1024 * 1024


def _rms_norm(x, g):
    ms = jnp.mean(x * x, axis=-1, keepdims=True)
    return x * lax.rsqrt(ms + EPS) * g


def _modulate(x, g, shift, scale):
    return _rms_norm(x, g) * (1 + scale) + shift


def _gelu(z):
    return 0.5 * z * (1 + lax.erf(z * (0.5 ** 0.5)))


def _layer_norm(v, g, b):
    mu = jnp.mean(v, axis=-1, keepdims=True)
    vc = v - mu
    var = jnp.mean(vc * vc, axis=-1, keepdims=True)
    return vc * lax.rsqrt(var + EPS) * g + b


def _resident(shape, layer):
    if len(shape) == 2:
        assert shape[0] == 1 and layer == 0
        return pl.BlockSpec(tuple(shape), lambda *_: (0, 0), pipeline_mode=pl.Buffered(1))
    return pl.BlockSpec((None,) + tuple(shape[1:]), lambda *_: (layer,) + (0,) * (len(shape) - 1),
                        pipeline_mode=pl.Buffered(1))


def _split_mod(mod, d):
    return [mod[:, k * d:(k + 1) * d] for k in range(N_MOD)]


def _channel_mlp(xn, w1_ref, w2_ref, a_ref):
    d_ff = w1_ref.shape[1]
    for c in range(d_ff // FF_COLS):
        cols = slice(c * FF_COLS, (c + 1) * FF_COLS)
        a = jnp.maximum(jnp.dot(xn, w1_ref[:, cols], preferred_element_type=F32), 0.0)
        a_ref[:, cols] = (a * a).astype(BF16)
    return jnp.dot(a_ref[...], w2_ref[...], preferred_element_type=F32)


def _pace(lhs_ref, v):
    bits = pltpu.bitcast(v[0:8, 0:128].astype(F32), jnp.int32)
    zero = lax.shift_right_logical(lax.shift_right_logical(bits, 16), 16).astype(F32)
    rows = 32 // lhs_ref.dtype.itemsize
    zero = jnp.concatenate([zero] * (rows // 8), axis=0).astype(lhs_ref.dtype)
    lhs_ref[0:rows, 0:128] = lhs_ref[0:rows, 0:128] + zero


def _channel_mlp_with_side_work(xn_ref, w1_ref, w2_ref, a_ref, side_work, emit):
    side = list(side_work)
    d, d_ff = w1_ref.shape

    def up(cols):
        a = jnp.maximum(jnp.dot(xn_ref[...], w1_ref[:, cols], preferred_element_type=F32), 0.0)
        a_ref[:, cols] = (a * a).astype(BF16)

    def down(cols):
        emit(cols, jnp.dot(a_ref[...], w2_ref[:, cols], preferred_element_type=F32))

    slots = [(xn_ref, up, slice(c * FF_COLS, (c + 1) * FF_COLS)) for c in range(d_ff // FF_COLS)]
    slots += [(a_ref, down, slice(c * DOWN_COLS, (c + 1) * DOWN_COLS)) for c in range(d // DOWN_COLS)]
    assert len(side) <= len(slots) - 2
    results = []
    for i, (lhs_ref, run, cols) in enumerate(slots):
        if i >= 2 and len(results) > i - 2:
            _pace(lhs_ref, results[i - 2])
        run(cols)
        if side:
            results.append(side.pop(0)())


def _pool_project(delta, wp_ref, ps_ref):
    ys = [jnp.dot(dl, wp_ref[g].astype(BF16), preferred_element_type=F32) for g, dl in enumerate(delta)]
    return jnp.concatenate(ys, axis=1) * ps_ref[...]


def _sgu_in(h, win_ref, bin_ref, z_ref):
    for c in range(win_ref.shape[1] // FF_COLS):
        cols = slice(c * FF_COLS, (c + 1) * FF_COLS)
        z = jnp.dot(h, win_ref[:, cols], preferred_element_type=F32) + bin_ref[:, cols]
        z_ref[:, cols] = _gelu(z)


def _ada_rows(c_ref, w_ref, b_row):
    s = jax.nn.silu(c_ref[...]).astype(BF16)
    return jnp.dot(s, w_ref[...].astype(BF16), preferred_element_type=F32) + b_row


def _first_layer_prep_kernel(layer, c_ref, w_ref, b_ref, *refs):
    n_cast = (len(refs) - 1) // 2
    o_ref = refs[n_cast]
    o_ref[...] = _ada_rows(c_ref, w_ref, b_ref[layer:layer + 1, :])
    for src, dst in zip(refs[:n_cast], refs[n_cast + 1:]):
        dst[...] = src[...].astype(BF16)


def _first_layer_prep(c_all, w_ada, b_ada, layer, weights):
    depth, d, n = w_ada.shape
    rows = c_all.shape[0]
    n_steps = n // ADA_COLS
    cast_in, cast_out, cast_shapes = [], [], []
    for w in weights:
        r = w.shape[1] // n_steps
        assert w.shape[1] % n_steps == 0 and r % 16 == 0
        cast_in.append(pl.BlockSpec((None, r, w.shape[2]), lambda j: (layer, j, 0)))
        cast_out.append(pl.BlockSpec((None, r, w.shape[2]), lambda j: (0, j, 0)))
        cast_shapes.append(jax.ShapeDtypeStruct((1,) + w.shape[1:], BF16))
    return pl.pallas_call(
        functools.partial(_first_layer_prep_kernel, layer),
        grid=(n_steps,),
        in_specs=[
            pl.BlockSpec((rows, d), lambda j: (0, 0)),
            pl.BlockSpec((None, d, ADA_COLS), lambda j: (layer, 0, j)),
            pl.BlockSpec((depth, ADA_COLS), lambda j: (0, j)),
        ] + cast_in,
        out_specs=[pl.BlockSpec((rows, ADA_COLS), lambda j: (0, j))] + cast_out,
        out_shape=[jax.ShapeDtypeStruct((rows, n), F32)] + cast_shapes,
        compiler_params=pltpu.CompilerParams(
            dimension_semantics=("arbitrary",), vmem_limit_bytes=VMEM_LIMIT),
        name="first_layer_prep",
    )(c_all, w_ada, b_ada, *weights)


def _pipelined_steps(n_tiles, init, mixer_pieces, mlp, sample_tile):
    s = pl.program_id(0)

    @pl.when(s == 0)
    def _():
        init()
        for piece in mixer_pieces(s):
            piece()

    @pl.when((s > 0) & (s <= n_tiles))
    def _():
        mlp(s - 1, mixer_pieces(jnp.minimum(s, n_tiles - 1)))

    if sample_tile is not None:
        @pl.when(s > n_tiles)
        def _():
            sample_tile()


def _handoff(xn_next, x1_next, xn_cur, x1_cur):
    xn_cur[...] = xn_next[...]
    x1_cur[...] = x1_next[...]


def _pipelined_tile_maps(n_seq, tiles_per_seq, n_sample_tiles):
    n_tiles = n_seq * tiles_per_seq

    def mixer_tile(s):
        t = jnp.minimum(s, n_tiles - 1)
        return t // tiles_per_seq, t % tiles_per_seq, 0

    def mlp_tile(s):
        t = jnp.clip(s - 1, 0, n_tiles - 1)
        return t // tiles_per_seq, t % tiles_per_seq, 0

    def sample_tile(s):
        return jnp.clip(s - n_tiles - 1, 0, n_sample_tiles - 1)

    return mixer_tile, mlp_tile, sample_tile, n_tiles + 1 + n_sample_tiles


def _pipeline_scratch(tm, d, d_ff):
    return [pltpu.VMEM((tm, d), BF16), pltpu.VMEM((tm, d), F32), pltpu.VMEM((tm, d), BF16),
            pltpu.VMEM((tm, d), F32), pltpu.VMEM((tm, d_ff), BF16)]


def _layer0_kernel(tiles_per_seq, n_tiles, next_layer, x_ref, mod_ref, ng_ref, wp_ref, ps_ref, w1_ref, w2_ref,
                   c_ref, wada_ref, bada_ref, xs_ref, bufs_ref, mods_ref, *rest):
    n_cast = (len(rest) - 11) // 2
    cast_src = rest[:n_cast]
    o_ref, tail_ref, mod_next_ref = rest[n_cast:n_cast + 3]
    cast_dst = rest[n_cast + 3:2 * n_cast + 3]
    os_ref, pools_ref = rest[2 * n_cast + 3:2 * n_cast + 5]
    hh_ref, xn_next, x1_next, xn_cur, x1_cur, a_ref = rest[2 * n_cast + 5:]
    tm, d = x_ref.shape
    pg = d // len(POOL_WINDOWS)

    def sample_tile():
        rows = xs_ref.shape[0] * xs_ref.shape[1]
        _layer0_sample_tile(xs_ref, bufs_ref, mods_ref, ng_ref, wp_ref, ps_ref, w1_ref, w2_ref,
                            os_ref, pools_ref, a_ref.at[pl.ds(0, rows)])

    def next_layer_prep():
        mod_next = _ada_rows(c_ref, wada_ref, bada_ref[next_layer:next_layer + 1, :])
        mod_next_ref[...] = mod_next
        for src, dst in zip(cast_src, cast_dst):
            dst[...] = src[...].astype(BF16)
        return mod_next

    def mixer_pieces(tile):
        b, j = tile // tiles_per_seq, tile % tiles_per_seq
        sh1, sc1, gt1, sh2, sc2, _ = _split_mod(mod_ref[pl.ds(b, 1), :], d)
        first = j == 0

        def norm():
            h = _modulate(x_ref[...], ng_ref[0:1, :], sh1, sc1)
            hh_ref[0:HIST] = jnp.where(first, 0.0, hh_ref[tm:tm + HIST])
            hh_ref[HIST:] = h
            for k in range(POOL_BUF):
                tail_ref[k, pl.ds(b, 1), :] = h[tm - POOL_BUF + k:tm - POOL_BUF + k + 1]
            return h

        def group(g):
            w = POOL_WINDOWS[g]
            cols = slice(g * pg, (g + 1) * pg)
            s = hh_ref[:, cols]
            span = 1
            while span < w:
                s = s + pltpu.roll(s, span, axis=0)
                span *= 2
            s = s[HIST:]
            r = lax.broadcasted_iota(jnp.int32, (HIST, pg), 0)
            cnt = jnp.where(first, jnp.minimum(w, r + 1), w).astype(F32)
            m = jnp.concatenate([s[:HIST] / cnt, s[HIST:] * (1.0 / w)], axis=0)
            delta = (m - hh_ref[HIST:, cols]).astype(BF16)
            y = jnp.dot(delta, wp_ref[g].astype(BF16), preferred_element_type=F32) * ps_ref[:, cols]
            x1 = x_ref[:, cols] + gt1[:, cols] * y
            x1_next[:, cols] = x1
            return x1

        def norm2():
            xn = _modulate(x1_next[...], ng_ref[1:2, :], sh2, sc2)
            xn_next[...] = xn.astype(BF16)
            return xn

        return ([norm] + [functools.partial(group, g) for g in range(len(POOL_WINDOWS))]
                + [norm2, next_layer_prep])

    def mlp(tile, side_work):
        gt2 = mod_ref[pl.ds(tile // tiles_per_seq, 1), (N_MOD - 1) * d:]
        _handoff(xn_next, x1_next, xn_cur, x1_cur)

        def emit(cols, y):
            o_ref[:, cols] = x1_cur[:, cols] + gt2[:, cols] * y

        _channel_mlp_with_side_work(xn_cur, w1_ref, w2_ref, a_ref, side_work, emit)

    def init():
        hh_ref[...] = jnp.zeros_like(hh_ref)

    _pipelined_steps(n_tiles, init, mixer_pieces, mlp, sample_tile)


def _layer0_sample_tile(x_ref, buf_ref, mod_ref, ng_ref, wp_ref, ps_ref, w1_ref, w2_ref,
                        o_ref, pool_ref, a_ref):
    n_seq, seq_rows, d = x_ref.shape
    sh1, sc1, gt1, sh2, sc2, gt2 = _split_mod(mod_ref[...], d)

    xt = jnp.swapaxes(x_ref[...], 0, 1)
    xs = [xt[t] for t in range(seq_rows)]
    hs = [_modulate(x, ng_ref[0:1, :], sh1, sc1) for x in xs]
    rows = [buf_ref[k] for k in range(POOL_BUF)] + hs
    for k in range(POOL_BUF):
        pool_ref[k] = rows[len(rows) - POOL_BUF + k]

    pg = d // len(POOL_WINDOWS)
    delta = []
    for g, w in enumerate(POOL_WINDOWS):
        cols = slice(g * pg, (g + 1) * pg)
        sums = {i: rows[i][:, cols] for i in range(len(rows))}
        span = 1
        while span < w:
            first = POOL_BUF - (w - 2 * span)
            sums = {i: sums[i] + sums[i - span] for i in range(first, len(rows))}
            span *= 2
        delta.append(jnp.concatenate(
            [(sums[POOL_BUF + t] * (1.0 / w) - hs[t][:, cols]).astype(BF16) for t in range(seq_rows)], axis=0))
    y = _pool_project(delta, wp_ref, ps_ref)
    x1 = [xs[t] + gt1 * y[t * n_seq:(t + 1) * n_seq] for t in range(seq_rows)]

    xn = jnp.concatenate([_modulate(x, ng_ref[1:2, :], sh2, sc2).astype(BF16) for x in x1], axis=0)
    y = _channel_mlp(xn, w1_ref, w2_ref, a_ref)
    out = [x1[t] + gt2 * y[t * n_seq:(t + 1) * n_seq] for t in range(seq_rows)]
    o_ref[...] = jnp.swapaxes(jnp.stack(out, axis=0), 0, 1)


def _layer0(x, xs, bufs, mod, layer, ng, wp, ps, w1, w2, c_all, w_ada, b_ada, next_weights):
    n_seq, seq, d = x.shape
    n_s, seq_rows, _ = xs.shape
    d_ff = w1.shape[2]
    tm = TILE_ROWS
    tb = SAMPLE_SEQS
    n_tiles = n_seq * (seq // tm)
    mixer_tile, mlp_tile, sample_tile, n_steps = _pipelined_tile_maps(n_seq, seq // tm, n_s // tb)
    assert tb * seq_rows <= tm and n_s % tb == 0 and n_s % n_seq == 0
    n_mod = w_ada.shape[2]
    assert n_mod % ADA_PREP_COLS == 0 and n_mod // ADA_PREP_COLS <= n_tiles
    ada_block = lambda s: jnp.minimum(s, n_mod // ADA_PREP_COLS - 1)
    cast_in, cast_out, cast_shapes = [], [], []
    for w, l in next_weights:
        rows = w.shape[1] // n_tiles
        assert w.shape[1] % n_tiles == 0 and rows % 16 == 0
        cast_in.append(pl.BlockSpec((None, rows, w.shape[2]),
                                    lambda s, l=l: (l, jnp.minimum(s, n_tiles - 1), 0)))
        cast_out.append(pl.BlockSpec((None, rows, w.shape[2]), lambda s: (0, jnp.minimum(s, n_tiles - 1), 0)))
        cast_shapes.append(jax.ShapeDtypeStruct((1,) + w.shape[1:], BF16))
    return pl.pallas_call(
        functools.partial(_layer0_kernel, seq // tm, n_tiles, layer + 1),
        grid=(n_steps,),
        in_specs=[
            pl.BlockSpec((None, tm, d), mixer_tile),
            pl.BlockSpec((n_seq, N_MOD * d), lambda s: (n_s // n_seq, 0)),
            _resident(ng.shape, layer), _resident(wp.shape, 0), _resident(ps.shape, 0),
            _resident(w1.shape, 0), _resident(w2.shape, 0),
            pl.BlockSpec(c_all.shape, lambda s: (0, 0)),
            pl.BlockSpec((None, d, ADA_PREP_COLS), lambda s: (layer + 1, 0, ada_block(s))),
            pl.BlockSpec((b_ada.shape[0], ADA_PREP_COLS), lambda s: (0, ada_block(s))),
            pl.BlockSpec((tb, seq_rows, d), lambda s: (sample_tile(s), 0, 0)),
            pl.BlockSpec((POOL_BUF, tb, d), lambda s: (0, sample_tile(s), 0)),
            pl.BlockSpec((tb, N_MOD * d), lambda s: (sample_tile(s), 0)),
        ] + cast_in,
        out_specs=[
            pl.BlockSpec((None, tm, d), mlp_tile),
            pl.BlockSpec((POOL_BUF, n_seq, d), lambda s: (0, 0, 0)),
            pl.BlockSpec((c_all.shape[0], ADA_PREP_COLS), lambda s: (0, ada_block(s))),
        ] + cast_out + [
            pl.BlockSpec((tb, seq_rows, d), lambda s: (sample_tile(s), 0, 0)),
            pl.BlockSpec((POOL_BUF, tb, d), lambda s: (0, sample_tile(s), 0)),
        ],
        out_shape=[
            jax.ShapeDtypeStruct((n_seq, seq, d), F32),
            jax.ShapeDtypeStruct((POOL_BUF, n_seq, d), F32),
            jax.ShapeDtypeStruct((c_all.shape[0], n_mod), F32),
        ] + cast_shapes + [jax.ShapeDtypeStruct(xs.shape, F32), jax.ShapeDtypeStruct(bufs.shape, F32)],
        scratch_shapes=[pltpu.VMEM((HIST + tm, d), F32)] + _pipeline_scratch(tm, d, d_ff),
        compiler_params=pltpu.CompilerParams(
            dimension_semantics=("arbitrary",), vmem_limit_bytes=VMEM_LIMIT),
        name="layer0",
    )(x, mod, ng, wp, ps, w1, w2, c_all, w_ada, b_ada, xs, bufs, mod, *[w for w, _ in next_weights])


def _layer1_prompt_kernel(tiles_per_seq, n_tiles, x_ref, mod_ref, ng_ref, win_ref, bin_ref, lng_ref, lnb_ref,
                          wsp_ref, bsp_ref, wout_ref, w1_ref, w2_ref, fg_ref, o_ref, v_ref,
                          hb_ref, z_ref, vnb_ref, gate_ref, xn_next, x1_next, xn_cur, x1_cur, a_ref):
    tm, d = x_ref.shape
    ds = wout_ref.shape[0]
    v_rows = v_ref.shape[0]
    gs = ds // N_SGU_GROUPS

    def mixer_pieces(tile):
        sh1, sc1, gt1, sh2, sc2, _ = _split_mod(mod_ref[pl.ds(tile // tiles_per_seq, 1), :], d)

        def norm():
            h = _modulate(x_ref[...], ng_ref[0:1, :], sh1, sc1)
            hb_ref[...] = h.astype(BF16)
            return h

        def sgu_in(c):
            cols = slice(c * FF_COLS, (c + 1) * FF_COLS)
            z = jnp.dot(hb_ref[...], win_ref[:, cols], preferred_element_type=F32) + bin_ref[:, cols]
            z = _gelu(z)
            z_ref[:, cols] = z
            return z

        def norm_v():
            vn = _layer_norm(z_ref[:, ds:], lng_ref[...], lnb_ref[...])
            v_ref[...] = vn[tm - v_rows:]
            vnb_ref[...] = vn.astype(BF16)
            return vn

        def spatial():
            r = lax.broadcasted_iota(jnp.int32, (CHUNK, CHUNK), 0)
            c = lax.broadcasted_iota(jnp.int32, (CHUNK, CHUNK), 1)
            chunks = [slice(ci * CHUNK, (ci + 1) * CHUNK) for ci in range(tm // CHUNK)]
            for g in range(N_SGU_GROUPS):
                wt = jnp.where(c <= r, wsp_ref[g], 0.0).astype(BF16)
                cols = slice(g * gs, (g + 1) * gs)
                v_all = jnp.concatenate([vnb_ref[rows, cols] for rows in chunks], axis=1)
                mixed_all = jnp.dot(wt, v_all, preferred_element_type=F32)
                for ci, rows in enumerate(chunks):
                    mixed = mixed_all[:, ci * gs:(ci + 1) * gs] + bsp_ref[:, cols]
                    gate_ref[rows, cols] = (z_ref[rows, cols] * mixed).astype(BF16)
            return mixed

        def sgu_out():
            x1 = x_ref[...] + gt1 * jnp.dot(gate_ref[...], wout_ref[...], preferred_element_type=F32)
            x1_next[...] = x1
            xn = _modulate(x1, ng_ref[1:2, :], sh2, sc2)
            xn_next[...] = xn.astype(BF16)
            return xn

        return ([norm] + [functools.partial(sgu_in, c) for c in range(2 * ds // FF_COLS)]
                + [norm_v, spatial, sgu_out])

    def mlp(tile, side_work):
        gt2 = mod_ref[pl.ds(tile // tiles_per_seq, 1), (N_MOD - 1) * d:]
        _handoff(xn_next, x1_next, xn_cur, x1_cur)

        def emit(cols, y):
            o_ref[:, cols] = x1_cur[:, cols] + gt2[:, cols] * y

        _channel_mlp_with_side_work(xn_cur, w1_ref, w2_ref, a_ref, side_work, emit)
        o_ref[...] = _rms_norm(o_ref[...], fg_ref[...])

    _pipelined_steps(n_tiles, lambda: None, mixer_pieces, mlp, None)


def _layer1_sample_tile(wsp_ref, x_ref, mod_ref, ng_ref, win_ref, bin_ref, lng_ref, lnb_ref,
                        bsp_ref, wout_ref, w1_ref, w2_ref, fg_ref, o_ref, v_ref, z_ref, a_ref):
    n_seq, seq_rows, d = x_ref.shape
    ds = wout_ref.shape[0]
    sh1, sc1, gt1, sh2, sc2, gt2 = _split_mod(mod_ref[...], d)

    xt = jnp.swapaxes(x_ref[...], 0, 1)
    xs = [xt[t] for t in range(seq_rows)]
    h = jnp.concatenate([_modulate(x, ng_ref[0:1, :], sh1, sc1).astype(BF16) for x in xs], axis=0)
    _sgu_in(h, win_ref, bin_ref, z_ref)
    vn = _layer_norm(z_ref[:, ds:], lng_ref[...], lnb_ref[...])
    vs = [vn[t * n_seq:(t + 1) * n_seq] for t in range(seq_rows)]
    v_ref[...] = jnp.swapaxes(vn.reshape(seq_rows, n_seq, ds), 0, 1)

    gs = ds // N_SGU_GROUPS
    gate = []
    for t in range(seq_rows):
        parts = []
        for g in range(N_SGU_GROUPS):
            cols = slice(g * gs, (g + 1) * gs)
            mixed = bsp_ref[t:t + 1, cols]
            for s in range(t + 1):
                mixed = mixed + wsp_ref[(g * seq_rows + t) * seq_rows + s] * vs[s][:, cols]
            parts.append(mixed)
        gate.append((z_ref[t * n_seq:(t + 1) * n_seq, :ds] * jnp.concatenate(parts, axis=1)).astype(BF16))
    y = jnp.dot(jnp.concatenate(gate, axis=0), wout_ref[...], preferred_element_type=F32)
    x1 = [xs[t] + gt1 * y[t * n_seq:(t + 1) * n_seq] for t in range(seq_rows)]

    xn = jnp.concatenate([_modulate(x, ng_ref[1:2, :], sh2, sc2).astype(BF16) for x in x1], axis=0)
    y = _channel_mlp(xn, w1_ref, w2_ref, a_ref)
    out = [_rms_norm(x1[t] + gt2 * y[t * n_seq:(t + 1) * n_seq], fg_ref[...]) for t in range(seq_rows)]
    o_ref[...] = jnp.swapaxes(jnp.stack(out, axis=0), 0, 1)


def _layer1_weight_specs(layer, ng, win, b_in, lng, lnb, bias, wout, w1, w2, fg):
    return [_resident(ng.shape, layer), _resident(win.shape, 0), _resident(b_in.shape, 0),
            _resident(lng.shape, 0), _resident(lnb.shape, 0), _resident(bias.shape, 0),
            _resident(wout.shape, 0), _resident(w1.shape, 0), _resident(w2.shape, 0),
            _resident(fg.shape, 0)]


def _layer1_prompt(x, mod, n_lead, layer, ng, win, b_in, lng, lnb, wsp, bias, wout, w1, w2, fg):
    n_seq, seq, d = x.shape
    ds = wout.shape[1]
    d_ff = w1.shape[2]
    tm = TILE_ROWS
    n_tiles = n_seq * (seq // tm)
    v_rows = seq - CHUNK * ((seq - 1) // CHUNK)
    specs = _layer1_weight_specs(layer, ng, win, b_in, lng, lnb, bias, wout, w1, w2, fg)
    mixer_tile, mlp_tile, _, n_steps = _pipelined_tile_maps(n_seq, seq // tm, 0)
    return pl.pallas_call(
        functools.partial(_layer1_prompt_kernel, seq // tm, n_tiles),
        grid=(n_steps,),
        in_specs=[
            pl.BlockSpec((None, tm, d), mixer_tile),
            pl.BlockSpec((n_seq, N_MOD * d), lambda s: (n_lead // n_seq, 0)),
        ] + specs[:5] + [_resident(wsp.shape, 0)] + specs[5:],
        out_specs=[
            pl.BlockSpec((None, tm, d), mlp_tile),
            pl.BlockSpec((None, v_rows, ds), lambda s: (mixer_tile(s)[0], 0, 0)),
        ],
        out_shape=[
            jax.ShapeDtypeStruct((n_seq, seq, d), F32),
            jax.ShapeDtypeStruct((n_seq, v_rows, ds), F32),
        ],
        scratch_shapes=[pltpu.VMEM((tm, d), BF16), pltpu.VMEM((tm, 2 * ds), F32), pltpu.VMEM((tm, ds), BF16),
                        pltpu.VMEM((tm, ds), BF16)] + _pipeline_scratch(tm, d, d_ff),
        compiler_params=pltpu.CompilerParams(
            dimension_semantics=("arbitrary",), vmem_limit_bytes=L1_VMEM_LIMIT),
        name="layer1_prompt",
    )(x, mod, ng, win, b_in, lng, lnb, wsp, bias, wout, w1, w2, fg)


def _layer1_sample(xs, mod, layer, wsp_corner, ng, win, b_in, lng, lnb, bias, wout, w1, w2, fg):
    n_seq, seq_rows, d = xs.shape
    ds = wout.shape[1]
    d_ff = w1.shape[2]
    tb = L1_SAMPLE_SEQS
    tm = tb * seq_rows
    assert n_seq % tb == 0
    return pl.pallas_call(
        _layer1_sample_tile,
        grid=(n_seq // tb,),
        in_specs=[
            pl.BlockSpec(memory_space=pltpu.SMEM),
            pl.BlockSpec((tb, seq_rows, d), lambda i: (i, 0, 0)),
            pl.BlockSpec((tb, N_MOD * d), lambda i: (i, 0)),
        ] + _layer1_weight_specs(layer, ng, win, b_in, lng, lnb, bias, wout, w1, w2, fg),
        out_specs=[pl.BlockSpec((tb, seq_rows, d), lambda i: (i, 0, 0)),
                   pl.BlockSpec((tb, seq_rows, ds), lambda i: (i, 0, 0))],
        out_shape=[jax.ShapeDtypeStruct(xs.shape, F32), jax.ShapeDtypeStruct((n_seq, seq_rows, ds), F32)],
        scratch_shapes=[pltpu.VMEM((tm, 2 * ds), F32), pltpu.VMEM((tm, d_ff), BF16)],
        compiler_params=pltpu.CompilerParams(
            dimension_semantics=("arbitrary",), vmem_limit_bytes=L1_VMEM_LIMIT),
        name="layer1_sample",
    )(wsp_corner, xs, mod, ng, win, b_in, lng, lnb, bias, wout, w1, w2, fg)


def kernel(x_prompt, x_sample, c_prompt, c_sample, state_pool, norm_g, w_ada, b_ada, w_pool, pool_scale,
           sgu_w_in, sgu_b_in, sgu_ln_g, sgu_ln_b, sgu_w_sp, sgu_b_sp, sgu_w_out, mlp_w1, mlp_w2, final_g):
    n_p, seq, d = x_prompt.shape
    n_s, dec_seq, _ = x_sample.shape
    assert norm_g.shape[0] == 2 and state_pool.shape[0] == 1 and sgu_w_in.shape[0] == 1
    assert state_pool.shape[2] == POOL_BUF and dec_seq <= POOL_BUF and dec_seq <= CHUNK
    assert seq % TILE_ROWS == 0 and TILE_ROWS % CHUNK == 0 and TILE_ROWS >= HIST
    assert n_s % SAMPLE_SEQS == 0 and n_s % n_p == 0

    c_all = jnp.concatenate([c_sample, c_prompt], axis=0)
    mod0, w1_0, w2_0 = _first_layer_prep(c_all, w_ada, b_ada, 0, [mlp_w1, mlp_w2])

    buf = jnp.transpose(state_pool[0], (1, 0, 2))
    xp, pool_p, mod1, w_in, w_out, w1, w2, xs, pool_s = _layer0(
        x_prompt, x_sample, buf, mod0, 0, norm_g, w_pool, pool_scale, w1_0, w2_0, c_all, w_ada, b_ada,
        [(sgu_w_in, 0), (sgu_w_out, 0), (mlp_w1, 1), (mlp_w2, 1)])

    ds = sgu_w_out.shape[1]
    bias = jnp.repeat(sgu_b_sp[0].T, ds // N_SGU_GROUPS, axis=1)[None]
    corner = sgu_w_sp[0][:, :dec_seq, :dec_seq].reshape(-1)
    l1 = (norm_g, w_in, sgu_b_in, sgu_ln_g, sgu_ln_b, bias, w_out, w1, w2, final_g[None])
    yp, v_p = _layer1_prompt(xp, mod1, n_s, 1, *l1[:5], sgu_w_sp, *l1[5:])
    ys, v_s = _layer1_sample(xs, mod1, 1, corner, *l1)

    return (yp, ys,
            jnp.transpose(pool_p, (1, 0, 2))[None], jnp.transpose(pool_s, (1, 0, 2))[None],
            v_p[None], v_s[None])
```
